```python
import math
import jax, jax.numpy as jnp
from jax import lax
import numpy as np

D_MODEL = 1024
BATCH = 8
SEQ = 4096
DEPTH = 4

CHUNK = 64
N_MIXERS = 3
EPS = 1e-6

GLA_HEADS = 4
GLA_DK = D_MODEL // 2
GLA_DV = D_MODEL
GLA_DKH = GLA_DK // GLA_HEADS
GLA_DVH = GLA_DV // GLA_HEADS
GLA_RANK = 16
GLA_TAU = 16.0
GLA_IN = 2 * GLA_DK + 2 * GLA_DV + GLA_RANK

CONV_WIDTH = 31

FOX_HEADS = 16
FOX_DH = D_MODEL // FOX_HEADS
FOX_IN = 4 * D_MODEL + FOX_HEADS
Q_BLOCK = 128

D_FF = 2816
N_EXPERTS = 8
TOP_K = 2
D_FF_EXPERT = 3584
MOE_BLOCK = 512

kernel_name = "hybrid_gla_conformer_fox_moe_adaln"


def rms_norm(x, g=None):
    xf = x.astype(jnp.float32)
    y = xf * lax.rsqrt(jnp.mean(xf * xf, axis=-1, keepdims=True) + EPS)
    if g is not None:
        y = y * g.astype(jnp.float32)
    return y.astype(x.dtype)


def gla_mixer(h, w_in, w_gate2, b_gate, gn_g, w_out):
    B, S, _ = h.shape
    nc = S // CHUNK
    f32 = jnp.float32
    q, k, v, r, glr = jnp.split(h @ w_in, [GLA_DK, 2 * GLA_DK, 2 * GLA_DK + GLA_DV,
                                          2 * GLA_DK + 2 * GLA_DV], axis=-1)
    log_a = jax.nn.log_sigmoid((glr @ w_gate2 + b_gate).astype(f32)) / GLA_TAU

    def to_chunks(t, dh):
        return jnp.moveaxis(t.reshape(B, nc, CHUNK, GLA_HEADS, dh), 1, 0)

    qc = to_chunks(q, GLA_DKH).astype(f32) * (GLA_DKH ** -0.5)
    kc = to_chunks(k, GLA_DKH).astype(f32)
    vc = to_chunks(v, GLA_DVH).astype(f32)
    cum = jnp.cumsum(to_chunks(log_a, GLA_DKH), axis=2)
    tot = cum[:, :, -1]
    kd = kc * jnp.exp(tot[:, :, None] - cum)

    def step(state, inp):
        q_i, kd_i, v_i, tot_i = inp
        state = state * jnp.exp(tot_i)[..., None] + jnp.einsum('bchk,bchv->bhkv', kd_i, v_i)
        o_i = jnp.einsum('bchk,bhkv->bchv', q_i, state)
        return state, o_i

    s0 = jnp.zeros((B, GLA_HEADS, GLA_DKH, GLA_DVH), f32)
    _, o = lax.scan(step, s0, (qc, kd, vc, tot))
    o = jnp.moveaxis(o, 0, 1).reshape(B, S, GLA_HEADS, GLA_DVH)
    o = rms_norm(o, gn_g.reshape(GLA_HEADS, GLA_DVH)).astype(h.dtype).reshape(B, S, GLA_DV)
    return (o * jax.nn.silu(r)) @ w_out


def conv_mixer(h, w_in, b_in, dw, dw_b, ln_g, ln_b, w_out, b_out):
    a, g = jnp.split(h @ w_in + b_in, 2, axis=-1)
    u = a * jax.nn.sigmoid(g)
    u = lax.conv_general_dilated(u, dw[:, None, :].astype(u.dtype), window_strides=(1,),
                                 padding=[(CONV_WIDTH - 1, 0)],
                                 dimension_numbers=('NWC', 'WIO', 'NWC'),
                                 feature_group_count=D_MODEL) + dw_b
    uf = u.astype(jnp.float32)
    mu = jnp.mean(uf, axis=-1, keepdims=True)
    var = jnp.mean(jnp.square(uf - mu), axis=-1, keepdims=True)
    u = ((uf - mu) * lax.rsqrt(var + EPS) * ln_g + ln_b).astype(h.dtype)
    return jax.nn.silu(u) @ w_out + b_out


def fox_mixer(h, w_in, b_f, qn_g, kn_g, w_out):
    B, S, D = h.shape
    f32 = jnp.float32
    q, k, v, fl, og = jnp.split(h @ w_in, [D, 2 * D, 3 * D, 3 * D + FOX_HEADS], axis=-1)
    q = rms_norm(q.reshape(B, S, FOX_HEADS, FOX_DH), qn_g)
    k = rms_norm(k.reshape(B, S, FOX_HEADS, FOX_DH), kn_g)
    v = v.reshape(B, S, FOX_HEADS, FOX_DH)
    cum = jnp.cumsum(jax.nn.log_sigmoid((fl + b_f).astype(f32)), axis=1)
    cum_k = jnp.transpose(cum, (0, 2, 1))
    nb = S // Q_BLOCK
    qb = jnp.moveaxis(q.reshape(B, nb, Q_BLOCK, FOX_HEADS, FOX_DH), 1, 0)
    cq = jnp.moveaxis(cum_k.reshape(B, FOX_HEADS, nb, Q_BLOCK), 2, 0)
    kpos = jnp.arange(S)
    scale = FOX_DH ** -0.5

    def block(args):
        q_i, c_i, i = args
        s = jnp.einsum('bqhd,bkhd->bhqk', q_i, k, preferred_element_type=f32) * scale
        s = s + (c_i[..., None] - cum_k[:, :, None, :])
        qpos = i * Q_BLOCK + jnp.arange(Q_BLOCK)
        s = jnp.where((kpos[None, :] <= qpos[:, None])[None, None], s, -jnp.inf)
        p = jax.nn.softmax(s, axis=-1)
        return jnp.einsum('bhqk,bkhd->bqhd', p.astype(v.dtype), v)

    o = lax.map(block, (qb, cq, jnp.arange(nb)))
    o = jnp.moveaxis(o, 0, 1).reshape(B, S, D)
    return (o * jax.nn.sigmoid(og)) @ w_out


def swiglu(h, w13, w2):
    a, b = jnp.split(h @ w13, 2, axis=-1)
    return (jax.nn.silu(a) * b) @ w2


def moe_swiglu(h, router, w13, w2):
    B, S, D = h.shape
    N = B * S
    xt = h.reshape(N, D)
    top_l, top_e = lax.top_k((xt @ router).astype(jnp.float32), TOP_K)
    gates = jax.nn.softmax(top_l, axis=-1)
    A = N * TOP_K
    e_flat = top_e.reshape(A)
    tok = jnp.arange(A, dtype=jnp.int32) // TOP_K
    order = jnp.argsort(e_flat)
    e_sorted = e_flat[order]
    counts = jnp.bincount(e_flat, length=N_EXPERTS)
    start = jnp.cumsum(counts) - counts
    padded = (counts + MOE_BLOCK - 1) // MOE_BLOCK * MOE_BLOCK
    pad_end = jnp.cumsum(padded)
    pad_start = pad_end - padded
    dest = pad_start[e_sorted] + jnp.arange(A, dtype=jnp.int32) - start[e_sorted]
    n_blocks = -(-A // MOE_BLOCK) + N_EXPERTS
    P = n_blocks * MOE_BLOCK
    slot_tok = jnp.zeros((P,), jnp.int32).at[dest].set(tok[order])
    slot_gate = jnp.zeros((P,), h.dtype).at[dest].set(gates.reshape(A)[order].astype(h.dtype))
    block_e = jnp.minimum(jnp.searchsorted(pad_end, jnp.arange(n_blocks, dtype=jnp.int32) * MOE_BLOCK,
                                           side='right'), N_EXPERTS - 1)

    def run_block(args):
        toks, g, e = args
        a, u = jnp.split(xt[toks] @ w13[e], 2, axis=-1)
        return ((jax.nn.silu(a) * u) @ w2[e]) * g[:, None]

    yb = lax.map(run_block, (slot_tok.reshape(n_blocks, MOE_BLOCK),
                             slot_gate.reshape(n_blocks, MOE_BLOCK), block_e))
    y = jnp.zeros_like(xt).at[slot_tok].add(yb.reshape(P, D))
    return y.reshape(B, S, D)


def setup_inputs(seed: int = 0) -> dict:
    key = jax.random.key(seed)
    keys = iter(jax.random.split(key, 48))
    D = D_MODEL
    n_gla = len(range(0, DEPTH, N_MIXERS))
    n_conv = len(range(1, DEPTH, N_MIXERS))
    n_fox = len(range(2, DEPTH, N_MIXERS))
    n_dense = len(range(0, DEPTH, 2))
    n_moe = len(range(1, DEPTH, 2))

    def nrm(shape, scale):
        return jax.random.normal(next(keys), shape, jnp.float32) * scale

    def gain(shape):
        return 1.0 + nrm(shape, 0.1)

    return {
        "x": nrm((BATCH, SEQ, D), 1.0),
        "c": nrm((BATCH, D), 1.0),
        "ada_w": nrm((DEPTH, D, 6 * D), 0.5 * D ** -0.5),
        "ada_b": nrm((DEPTH, 6 * D), 0.02),
        "gla_w_in": nrm((n_gla, D, GLA_IN), D ** -0.5),
        "gla_w_gate2": nrm((n_gla, GLA_RANK, GLA_DK), GLA_RANK ** -0.5),
        "gla_b_gate": 1.5 + nrm((n_gla, GLA_DK), 0.5),
        "gla_gn_g": gain((n_gla, GLA_DV)),
        "gla_w_out": nrm((n_gla, GLA_DV, D), GLA_DV ** -0.5),
        "conv_w_in": nrm((n_conv, D, 2 * D), D ** -0.5),
        "conv_b_in": nrm((n_conv, 2 * D), 0.02),
        "conv_dw": nrm((n_conv, CONV_WIDTH, D), CONV_WIDTH ** -0.5),
        "conv_dw_b": nrm((n_conv, D), 0.02),
        "conv_ln_g": gain((n_conv, D)),
        "conv_ln_b": nrm((n_conv, D), 0.02),
        "conv_w_out": nrm((n_conv, D, D), D ** -0.5),
        "conv_b_out": nrm((n_conv, D), 0.02),
        "fox_w_in": nrm((n_fox, D, FOX_IN), D ** -0.5),
        "fox_b_f": jax.random.uniform(next(keys), (n_fox, FOX_HEADS), jnp.float32, 1.0, 5.0),
        "fox_qn_g": gain((n_fox, FOX_DH)),
        "fox_kn_g": gain((n_fox, FOX_DH)),
        "fox_w_out": nrm((n_fox, D, D), D ** -0.5),
        "ffn_w13": nrm((n_dense, D, 2 * D_FF), D ** -0.5),
        "ffn_w2": nrm((n_dense, D_FF, D), D_FF ** -0.5),
        "moe_router": nrm((n_moe, D, N_EXPERTS), D ** -0.5),
        "moe_w13": nrm((n_moe, N_EXPERTS, D, 2 * D_FF_EXPERT), D ** -0.5),
        "moe_w2": nrm((n_moe, N_EXPERTS, D_FF_EXPERT, D), D_FF_EXPERT ** -0.5),
        "norm_f_g": gain((D,)),
    }


def reference(x, c, ada_w, ada_b,
              gla_w_in, gla_w_gate2, gla_b_gate, gla_gn_g, gla_w_out,
              conv_w_in, conv_b_in, conv_dw, conv_dw_b, conv_ln_g, conv_ln_b, conv_w_out, conv_b_out,
              fox_w_in, fox_b_f, fox_qn_g, fox_kn_g, fox_w_out,
              ffn_w13, ffn_w2, moe_router, moe_w13, moe_w2, norm_f_g):
    cond = jax.nn.silu(c)
    for i in range(DEPTH):
        mod = (cond @ ada_w[i] + ada_b[i])[:, None, :]
        sh1, sc1, g1, sh2, sc2, g2 = jnp.split(mod, 6, axis=-1)
        h = rms_norm(x) * (1 + sc1) + sh1
        m, j = i % N_MIXERS, i // N_MIXERS
        if m == 0:
            y = gla_mixer(h, gla_w_in[j], gla_w_gate2[j], gla_b_gate[j], gla_gn_g[j], gla_w_out[j])
        elif m == 1:
            y = conv_mixer(h, conv_w_in[j], conv_b_in[j], conv_dw[j], conv_dw_b[j],
                           conv_ln_g[j], conv_ln_b[j], conv_w_out[j], conv_b_out[j])
        else:
            y = fox_mixer(h, fox_w_in[j], fox_b_f[j], fox_qn_g[j], fox_kn_g[j], fox_w_out[j])
        x = x + g1 * y
        h = rms_norm(x) * (1 + sc2) + sh2
        if i % 2 == 0:
            y = swiglu(h, ffn_w13[i // 2], ffn_w2[i // 2])
        else:
            y = moe_swiglu(h, moe_router[i // 2], moe_w13[i // 2], moe_w2[i // 2])
        x = x + g2 * y
    return rms_norm(x, norm_f_g)
```

```python
import functools

import jax
import jax.numpy as jnp
from jax import lax
from jax.experimental import pallas as pl
from jax.experimental.pallas import tpu as pltpu

F32 = jnp.float32
BF16 = jnp.bfloat16
HIGHEST = lax.Precision.HIGHEST

D = 1024
EPS = 1e-6
CHUNK = 64
GLA_HEADS = 4
GLA_DK = 512
GLA_DV = 1024
GLA_DKH = 128
GLA_DVH = 256
GLA_RANK = 16
GLA_TAU = 16.0
CONV_WIDTH = 31
CONV_HALO = 32
FOX_HEADS = 16
FOX_DH = 64
N_EXPERTS = 8
D_FF = 2816
D_FF_EXPERT = 3584
LANES = 128
NEG_BIG = -1e30
VMEM_LIMIT = 56 * 1024 * 1024

ROW_TILE = 512
MOE_TILE = 512
FF_SUB = 256


def _cparams(sem):
    return pltpu.CompilerParams(dimension_semantics=sem, vmem_limit_bytes=VMEM_LIMIT)


def _dot(a, b):
    return jnp.dot(a, b, preferred_element_type=F32)


def _dot_nt(a, b):
    return lax.dot_general(a, b, (((1,), (1,)), ((), ())), preferred_element_type=F32)


def _dot_tn(a, b):
    return lax.dot_general(a, b, (((0,), (0,)), ((), ())), preferred_element_type=F32)


def _sigmoid(x):
    return 1.0 / (1.0 + jnp.exp(-x))


def _log_sigmoid(z):
    return jnp.minimum(z, 0.0) - jnp.log(1.0 + jnp.exp(-jnp.abs(z)))


def _norm_mod(x, sc, sh):
    ms = jnp.mean(x * x, axis=-1, keepdims=True)
    return x * lax.rsqrt(ms + EPS) * (1.0 + sc) + sh


def _split3(x):
    hi = x.astype(BF16)
    r1 = x - hi.astype(F32)
    mid = r1.astype(BF16)
    lo = (r1 - mid.astype(F32)).astype(BF16)
    return hi, mid, lo


def _tri_cumsum(tri, x):
    hi, mid, lo = _split3(x)
    return _dot(tri, hi) + _dot(tri, mid) + _dot(tri, lo)


def _lower_tri(n):
    r = lax.broadcasted_iota(jnp.int32, (n, n), 0)
    c = lax.broadcasted_iota(jnp.int32, (n, n), 1)
    return jnp.where(r >= c, 1.0, 0.0).astype(BF16)


def _ada_kernel(c_ref, w_ref, b_ref, o_ref):
    c = c_ref[...]
    cond = c * _sigmoid(c)
    o_ref[0, 0] = jnp.dot(cond, w_ref[0], precision=HIGHEST,
                          preferred_element_type=F32) + b_ref[0, 0]


def _ada_mod(c, ada_w, ada_b):
    depth = ada_w.shape[0]
    B = c.shape[0]
    return pl.pallas_call(
        _ada_kernel,
        grid=(depth, 6),
        in_specs=[
            pl.BlockSpec((B, D), lambda l, j: (0, 0)),
            pl.BlockSpec((1, D, D), lambda l, j: (l, 0, j)),
            pl.BlockSpec((1, 1, 1, D), lambda l, j: (l, j, 0, 0)),
        ],
        out_specs=pl.BlockSpec((1, 1, B, D), lambda l, j: (l, j, 0, 0)),
        out_shape=jax.ShapeDtypeStruct((depth, 6, B, D), F32),
        compiler_params=_cparams(("arbitrary", "arbitrary")),
        name="ada_mod",
    )(c, ada_w, ada_b.reshape(depth, 6, 1, D))


def _row_spec(width, col=0):
    return pl.BlockSpec((1, ROW_TILE, width), lambda b, i: (b, i, col))


def _vec_spec(width=D):
    return pl.BlockSpec((1, 1, width), lambda b, i: (b, 0, 0))


def _full_spec(shape):
    return pl.BlockSpec(shape, lambda b, i: (0,) * len(shape))


def _gla_proj_kernel(x_ref, sc_ref, sh_ref, w_ref, wg_ref, w2_ref, bg_ref, o_ref, la_ref):
    h = _norm_mod(x_ref[0], sc_ref[0], sh_ref[0]).astype(BF16)
    n_out = o_ref.shape[2]
    for c in range(0, n_out, 512):
        o_ref[0, :, c:c + 512] = _dot(h, w_ref[:, c:c + 512]).astype(BF16)
    glr = _dot(h, wg_ref[...]).astype(BF16)
    z = _dot(glr, w2_ref[...]) + bg_ref[...]
    la_ref[0] = _log_sigmoid(z) * (1.0 / GLA_TAU)


def _gla_proj(x, sc, sh, w_main, w_glr, w_g2, b_gate):
    B, S, _ = x.shape
    n_out = w_main.shape[1]
    return pl.pallas_call(
        _gla_proj_kernel,
        grid=(B, S // ROW_TILE),
        in_specs=[_row_spec(D), _vec_spec(), _vec_spec(),
                  _full_spec((D, n_out)), _full_spec((D, LANES)),
                  _full_spec((LANES, GLA_DK)), _full_spec((1, GLA_DK))],
        out_specs=[_row_spec(n_out), _row_spec(GLA_DK)],
        out_shape=[jax.ShapeDtypeStruct((B, S, n_out), BF16),
                   jax.ShapeDtypeStruct((B, S, GLA_DK), F32)],
        compiler_params=_cparams(("parallel", "parallel")),
        name="gla_proj",
    )(x, sc, sh, w_main, w_glr, w_g2, b_gate)


def _conv_proj_kernel(x_ref, sc_ref, sh_ref, w_ref, b_ref, o_ref):
    h = _norm_mod(x_ref[0], sc_ref[0], sh_ref[0]).astype(BF16)
    for c in range(0, D, 512):
        a = _dot(h, w_ref[:, c:c + 512]) + b_ref[:, c:c + 512]
        g = _dot(h, w_ref[:, D + c:D + c + 512]) + b_ref[:, D + c:D + c + 512]
        o_ref[0, :, c:c + 512] = (a * _sigmoid(g)).astype(BF16)


def _conv_proj(x, sc, sh, w_in, b_in):
    B, S, _ = x.shape
    return pl.pallas_call(
        _conv_proj_kernel,
        grid=(B, S // ROW_TILE),
        in_specs=[_row_spec(D), _vec_spec(), _vec_spec(),
                  _full_spec((D, 2 * D)), _full_spec((1, 2 * D))],
        out_specs=_row_spec(D),
        out_shape=jax.ShapeDtypeStruct((B, S, D), BF16),
        compiler_params=_cparams(("parallel", "parallel")),
        name="conv_proj",
    )(x, sc, sh, w_in, b_in)


def _fox_proj_kernel(x_ref, sc_ref, sh_ref, w_ref, wf_ref, bf_ref, o_ref, lf_ref):
    h = _norm_mod(x_ref[0], sc_ref[0], sh_ref[0]).astype(BF16)
    n_out = o_ref.shape[2]
    for c in range(0, n_out, 512):
        o_ref[0, :, c:c + 512] = _dot(h, w_ref[:, c:c + 512]).astype(BF16)
    lf_ref[0] = _log_sigmoid(_dot(h, wf_ref[...]) + bf_ref[...])


def _fox_proj(x, sc, sh, w_main, w_fl, b_f):
    B, S, _ = x.shape
    n_out = w_main.shape[1]
    return pl.pallas_call(
        _fox_proj_kernel,
        grid=(B, S // ROW_TILE),
        in_specs=[_row_spec(D), _vec_spec(), _vec_spec(),
                  _full_spec((D, n_out)), _full_spec((D, LANES)), _full_spec((1, LANES))],
        out_specs=[_row_spec(n_out), _row_spec(LANES)],
        out_shape=[jax.ShapeDtypeStruct((B, S, n_out), BF16),
                   jax.ShapeDtypeStruct((B, S, LANES), F32)],
        compiler_params=_cparams(("parallel", "parallel")),
        name="fox_proj",
    )(x, sc, sh, w_main, w_fl, b_f)


def _out_proj_kernel(y_ref, w_ref, b_ref, x_ref, g_ref, o_ref):
    for c in range(0, D, 512):
        y = _dot(y_ref[0], w_ref[:, c:c + 512]) + b_ref[:, c:c + 512]
        o_ref[0, :, c:c + 512] = x_ref[0, :, c:c + 512] + g_ref[0][:, c:c + 512] * y


def _out_proj(y, w, b, x, g):
    B, S, _ = x.shape
    return pl.pallas_call(
        _out_proj_kernel,
        grid=(B, S // ROW_TILE),
        in_specs=[_row_spec(D), _full_spec((D, D)), _full_spec((1, D)), _row_spec(D), _vec_spec()],
        out_specs=_row_spec(D),
        out_shape=jax.ShapeDtypeStruct((B, S, D), F32),
        compiler_params=_cparams(("parallel", "parallel")),
        name="out_proj",
    )(y, w, b, x, g)


def _gla_scan_kernel(q_ref, k_ref, v_ref, r_ref, la_ref, gn_ref, o_ref, st_ref):
    @pl.when(pl.program_id(2) == 0)
    def _():
        st_ref[...] = jnp.zeros_like(st_ref)

    tri = _lower_tri(CHUNK)
    scale = GLA_DKH ** -0.5
    st = st_ref[...]
    for c in range(ROW_TILE // CHUNK):
        sl = slice(c * CHUNK, (c + 1) * CHUNK)
        cum = _tri_cumsum(tri, la_ref[0, sl, :])
        tot = cum[CHUNK - 1:CHUNK, :]
        kd = (k_ref[0, sl, :].astype(F32) * jnp.exp(tot - cum)).astype(BF16)
        st = st * jnp.exp(tot) + _dot_tn(v_ref[0, sl, :], kd)
        o = _dot_nt(q_ref[0, sl, :], st.astype(BF16)) * scale
        ms = jnp.mean(o * o, axis=-1, keepdims=True)
        on = o * lax.rsqrt(ms + EPS) * gn_ref[0]
        r = r_ref[0, sl, :].astype(F32)
        o_ref[0, sl, :] = (on * (r * _sigmoid(r))).astype(BF16)
    st_ref[...] = st


def _gla_scan(qkvr, log_a, gn_g):
    B, S, _ = qkvr.shape
    kq = GLA_DK // GLA_DKH
    kv = (2 * GLA_DK) // GLA_DVH
    kr = kv + GLA_DV // GLA_DVH
    return pl.pallas_call(
        _gla_scan_kernel,
        grid=(B, GLA_HEADS, S // ROW_TILE),
        in_specs=[
            pl.BlockSpec((1, ROW_TILE, GLA_DKH), lambda b, h, i: (b, i, h)),
            pl.BlockSpec((1, ROW_TILE, GLA_DKH), lambda b, h, i: (b, i, kq + h)),
            pl.BlockSpec((1, ROW_TILE, GLA_DVH), lambda b, h, i: (b, i, kv + h)),
            pl.BlockSpec((1, ROW_TILE, GLA_DVH), lambda b, h, i: (b, i, kr + h)),
            pl.BlockSpec((1, ROW_TILE, GLA_DKH), lambda b, h, i: (b, i, h)),
            pl.BlockSpec((1, 1, GLA_DVH), lambda b, h, i: (h, 0, 0)),
        ],
        out_specs=pl.BlockSpec((1, ROW_TILE, GLA_DVH), lambda b, h, i: (b, i, h)),
        out_shape=jax.ShapeDtypeStruct((B, S, GLA_DV), BF16),
        scratch_shapes=[pltpu.VMEM((GLA_DVH, GLA_DKH), F32)],
        compiler_params=_cparams(("parallel", "parallel", "arbitrary")),
        name="gla_scan",
    )(qkvr, qkvr, qkvr, qkvr, log_a, gn_g.reshape(GLA_HEADS, 1, GLA_DVH))


CONV_ROWS = 64
CONV_COLS = 256


def _conv_post_kernel(uc_ref, up_ref, dw_ref, dwb_ref, lng_ref, lnb_ref, w_ref, bo_ref,
                      x_ref, g_ref, o_ref, ext_ref, cv_ref):
    first = pl.program_id(1) == 0
    halo = up_ref[0].astype(F32)
    ext_ref[0:CONV_HALO, :] = jnp.where(first, 0.0, halo)
    ext_ref[CONV_HALO:, :] = uc_ref[0].astype(F32)
    lead = CONV_HALO - (CONV_WIDTH - 1)

    for c in range(0, D, CONV_COLS):
        def row_group(r, carry, c=c):
            r0 = pl.multiple_of(r * CONV_ROWS, CONV_ROWS)
            win = ext_ref[pl.ds(r0, CONV_ROWS + CONV_HALO), c:c + CONV_COLS]
            acc = jnp.zeros((CONV_ROWS, CONV_COLS), F32) + dwb_ref[:, c:c + CONV_COLS]
            for w in range(CONV_WIDTH):
                acc = acc + dw_ref[w:w + 1, c:c + CONV_COLS] * win[lead + w:lead + w + CONV_ROWS, :]
            cv_ref[pl.ds(r0, CONV_ROWS), c:c + CONV_COLS] = acc
            return carry
        lax.fori_loop(0, ROW_TILE // CONV_ROWS, row_group, 0)

    u = cv_ref[...]
    mu = jnp.mean(u, axis=-1, keepdims=True)
    var = jnp.mean(jnp.square(u - mu), axis=-1, keepdims=True)
    un = (u - mu) * lax.rsqrt(var + EPS) * lng_ref[...] + lnb_ref[...]
    y = (un * _sigmoid(un)).astype(BF16)
    for c in range(0, D, 512):
        z = _dot(y, w_ref[:, c:c + 512]) + bo_ref[:, c:c + 512]
        o_ref[0, :, c:c + 512] = x_ref[0, :, c:c + 512] + g_ref[0][:, c:c + 512] * z


def _conv_post(u, dw, dw_b, ln_g, ln_b, w_out, b_out, x, g):
    B, S, _ = x.shape
    halo_blocks = ROW_TILE // CONV_HALO
    return pl.pallas_call(
        _conv_post_kernel,
        grid=(B, S // ROW_TILE),
        in_specs=[
            _row_spec(D),
            pl.BlockSpec((1, CONV_HALO, D), lambda b, i: (b, jnp.maximum(i * halo_blocks - 1, 0), 0)),
            _full_spec((CONV_WIDTH, D)), _full_spec((1, D)), _full_spec((1, D)), _full_spec((1, D)),
            _full_spec((D, D)), _full_spec((1, D)), _row_spec(D), _vec_spec(),
        ],
        out_specs=_row_spec(D),
        out_shape=jax.ShapeDtypeStruct((B, S, D), F32),
        scratch_shapes=[pltpu.VMEM((ROW_TILE + CONV_HALO, D), F32), pltpu.VMEM((ROW_TILE, D), F32)],
        compiler_params=_cparams(("parallel", "parallel")),
        name="conv_post",
    )(u, u, dw, dw_b, ln_g, ln_b, w_out, b_out, x, g)


CUM_TILE = 256


def _fox_cum_kernel(lf_ref, o_ref, carry_ref):
    @pl.when(pl.program_id(1) == 0)
    def _():
        carry_ref[...] = jnp.zeros_like(carry_ref)

    cum = _tri_cumsum(_lower_tri(CUM_TILE), lf_ref[0]) + carry_ref[...]
    o_ref[0] = cum
    carry_ref[...] = cum[CUM_TILE - 1:CUM_TILE, :]


def _fox_cum(lf):
    B, S, _ = lf.shape
    spec = pl.BlockSpec((1, CUM_TILE, LANES), lambda b, i: (b, i, 0))
    return pl.pallas_call(
        _fox_cum_kernel,
        grid=(B, S // CUM_TILE),
        in_specs=[spec],
        out_specs=spec,
        out_shape=jax.ShapeDtypeStruct((B, S, LANES), F32),
        scratch_shapes=[pltpu.VMEM((1, LANES), F32)],
        compiler_params=_cparams(("parallel", "arbitrary")),
        name="fox_cum",
    )(lf)


def _fox_prep_kernel(q_ref, k_ref, cum_ref, qg_ref, kg_ref, qa_ref, ka_ref):
    pair = pl.program_id(1)
    lane = lax.broadcasted_iota(jnp.int32, (1, LANES), 1)
    cum = cum_ref[0]
    q = q_ref[0].astype(F32)
    k = k_ref[0].astype(F32)
    for hh in range(2):
        in_head = (lane >= hh * FOX_DH) & (lane < (hh + 1) * FOX_DH)
        c = jnp.sum(jnp.where(lane == 2 * pair + hh, cum, 0.0), axis=-1, keepdims=True)
        c_hi, c_mid, c_lo = [p.astype(F32) for p in _split3(c)]

        def head_norm(t, g_ref):
            ms = jnp.sum(jnp.where(in_head, t * t, 0.0), axis=-1, keepdims=True) * (1.0 / FOX_DH)
            tn = t * lax.rsqrt(ms + EPS) * g_ref[...]
            return tn if hh == 0 else pltpu.roll(tn, FOX_DH, axis=1)

        qn = head_norm(q, qg_ref) * (FOX_DH ** -0.5)
        kn = head_norm(k, kg_ref)
        d = FOX_DH
        qa = jnp.where(lane < d, qn,
             jnp.where(lane == d, c_hi,
             jnp.where(lane == d + 1, c_mid,
             jnp.where(lane == d + 2, c_lo,
             jnp.where(lane < d + 6, 1.0, 0.0)))))
        ka = jnp.where(lane < d, kn,
             jnp.where(lane < d + 3, 1.0,
             jnp.where(lane == d + 3, -c_hi,
             jnp.where(lane == d + 4, -c_mid,
             jnp.where(lane == d + 5, -c_lo, 0.0)))))
        qa_ref[0, hh] = qa.astype(BF16)
        ka_ref[0, hh] = ka.astype(BF16)


def _fox_prep(qkvo, cum, qn_g, kn_g):
    B, S, _ = qkvo.shape
    pairs = FOX_HEADS // 2
    g2 = lambda g: jnp.concatenate([g, g]).reshape(1, LANES)
    aug = pl.BlockSpec((1, 2, ROW_TILE, LANES), lambda b, p, i: (b, p, i, 0))
    return pl.pallas_call(
        _fox_prep_kernel,
        grid=(B, pairs, S // ROW_TILE),
        in_specs=[
            pl.BlockSpec((1, ROW_TILE, LANES), lambda b, p, i: (b, i, p)),
            pl.BlockSpec((1, ROW_TILE, LANES), lambda b, p, i: (b, i, pairs + p)),
            pl.BlockSpec((1, ROW_TILE, LANES), lambda b, p, i: (b, i, 0)),
            pl.BlockSpec((1, LANES), lambda b, p, i: (0, 0)),
            pl.BlockSpec((1, LANES), lambda b, p, i: (0, 0)),
        ],
        out_specs=[aug, aug],
        out_shape=[jax.ShapeDtypeStruct((B, FOX_HEADS, S, LANES), BF16)] * 2,
        compiler_params=_cparams(("parallel", "parallel", "parallel")),
        name="fox_prep",
    )(qkvo, qkvo, cum, g2(qn_g), g2(kn_g))


ATT_TILE = 512


def _fox_flash_kernel(qa_ref, ka_ref, v_ref, og_ref, o_ref):
    qi = pl.program_id(2)
    T = ATT_TILE
    lane = lax.broadcasted_iota(jnp.int32, (1, LANES), 1)
    row = lax.broadcasted_iota(jnp.int32, (T, T), 0)
    col = lax.broadcasted_iota(jnp.int32, (T, T), 1)

    def one_head(hh):
        q = qa_ref[0, hh]

        def update(carry, s, kb):
            m, l, acc = carry
            m_new = jnp.maximum(m, jnp.max(s, axis=-1, keepdims=True))
            alpha = jnp.exp(m - m_new)
            p = jnp.exp(s - m_new)
            l = alpha * l + jnp.sum(p, axis=-1, keepdims=True)
            v = v_ref[0, pl.ds(pl.multiple_of(kb * T, T), T), :]
            acc = alpha * acc + _dot(p.astype(BF16), v)
            return m_new, l, acc

        def step(kb, carry):
            k = ka_ref[0, hh, pl.ds(pl.multiple_of(kb * T, T), T), :]
            return update(carry, _dot_nt(q, k), kb)

        init = (jnp.full((T, 1), NEG_BIG, F32), jnp.zeros((T, 1), F32), jnp.zeros((T, LANES), F32))
        carry = lax.fori_loop(0, qi, step, init)
        k = ka_ref[0, hh, pl.ds(pl.multiple_of(qi * T, T), T), :]
        s = jnp.where(col <= row, _dot_nt(q, k), NEG_BIG)
        _, l, acc = update(carry, s, qi)
        return acc / l

    o = jnp.where(lane < FOX_DH, one_head(0), one_head(1))
    o_ref[0] = (o * _sigmoid(og_ref[0].astype(F32))).astype(BF16)


def _fox_flash(qa, ka, qkvo):
    B, _, S, _ = qa.shape
    pairs = FOX_HEADS // 2
    v_off = 2 * pairs
    og_off = 3 * pairs
    return pl.pallas_call(
        _fox_flash_kernel,
        grid=(B, pairs, S // ATT_TILE),
        in_specs=[
            pl.BlockSpec((1, 2, ATT_TILE, LANES), lambda b, p, i: (b, p, i, 0)),
            pl.BlockSpec((1, 2, S, LANES), lambda b, p, i: (b, p, 0, 0)),
            pl.BlockSpec((1, S, LANES), lambda b, p, i: (b, 0, v_off + p)),
            pl.BlockSpec((1, ATT_TILE, LANES), lambda b, p, i: (b, i, og_off + p)),
        ],
        out_specs=pl.BlockSpec((1, ATT_TILE, LANES), lambda b, p, i: (b, i, p)),
        out_shape=jax.ShapeDtypeStruct((B, S, D), BF16),
        compiler_params=_cparams(("parallel", "parallel", "arbitrary")),
        name="fox_flash",
    )(qa, ka, qkvo, qkvo)


def _swiglu_acc(h, w1_ref, w3_ref, w2_ref, acc_ref):
    tf = w2_ref.shape[0]
    for c in range(0, tf, FF_SUB):
        e = min(c + FF_SUB, tf)
        a = _dot(h, w1_ref[:, c:e])
        b = _dot(h, w3_ref[:, c:e])
        act = (a * _sigmoid(a) * b).astype(BF16)
        acc_ref[...] += _dot(act, w2_ref[c:e, :])


def _dense_ffn_kernel(x_ref, sc_ref, sh_ref, g_ref, w1_ref, w3_ref, w2_ref, o_ref, h_ref, acc_ref):
    f = pl.program_id(2)

    @pl.when(f == 0)
    def _():
        h_ref[...] = _norm_mod(x_ref[0], sc_ref[0], sh_ref[0]).astype(BF16)
        acc_ref[...] = jnp.zeros_like(acc_ref)

    _swiglu_acc(h_ref[...], w1_ref, w3_ref, w2_ref, acc_ref)

    @pl.when(f == pl.num_programs(2) - 1)
    def _():
        o_ref[0] = x_ref[0] + g_ref[0] * acc_ref[...]


DENSE_FF_TILE = 1408


def _dense_ffn(x, sc, sh, g, w13, w2):
    B, S, _ = x.shape
    tf = DENSE_FF_TILE
    nf = D_FF // tf
    row = pl.BlockSpec((1, ROW_TILE, D), lambda b, i, f: (b, i, 0))
    vec = pl.BlockSpec((1, 1, D), lambda b, i, f: (b, 0, 0))
    return pl.pallas_call(
        _dense_ffn_kernel,
        grid=(B, S // ROW_TILE, nf),
        in_specs=[row, vec, vec, vec,
                  pl.BlockSpec((D, tf), lambda b, i, f: (0, f)),
                  pl.BlockSpec((D, tf), lambda b, i, f: (0, nf + f)),
                  pl.BlockSpec((tf, D), lambda b, i, f: (f, 0))],
        out_specs=row,
        out_shape=jax.ShapeDtypeStruct((B, S, D), F32),
        scratch_shapes=[pltpu.VMEM((ROW_TILE, D), BF16), pltpu.VMEM((ROW_TILE, D), F32)],
        compiler_params=_cparams(("parallel", "parallel", "arbitrary")),
        name="dense_ffn",
    )(x, sc, sh, g, w13, w13, w2)


def _moe_ffn_kernel(be_ref, nb_ref, x_ref, gate_ref, w1_ref, w3_ref, w2_ref, o_ref, acc_ref):
    i = pl.program_id(0)
    f = pl.program_id(1)
    used = i < nb_ref[0]

    @pl.when(f == 0)
    def _():
        acc_ref[...] = jnp.zeros_like(acc_ref)

    @pl.when(used)
    def _():
        _swiglu_acc(x_ref[...], w1_ref.at[0], w3_ref.at[0], w2_ref.at[0], acc_ref)

    @pl.when(f == pl.num_programs(1) - 1)
    def _():
        o_ref[...] = acc_ref[...] * gate_ref[...]


MOE_FF_TILE = 1792


def _moe_ffn(block_e, n_used, xg, slot_gate, w13, w2):
    P = xg.shape[0]
    tf = MOE_FF_TILE
    nf = D_FF_EXPERT // tf
    last = nf - 1

    def held(i, nb):
        return jnp.minimum(i, nb[0] - 1)

    def fcol(i, f, nb):
        return jnp.where(i < nb[0], f, last)

    grid_spec = pltpu.PrefetchScalarGridSpec(
        num_scalar_prefetch=2,
        grid=(P // MOE_TILE, nf),
        in_specs=[
            pl.BlockSpec((MOE_TILE, D), lambda i, f, be, nb: (held(i, nb), 0)),
            pl.BlockSpec((MOE_TILE, 1), lambda i, f, be, nb: (i, 0)),
            pl.BlockSpec((1, D, tf), lambda i, f, be, nb: (be[i], 0, fcol(i, f, nb))),
            pl.BlockSpec((1, D, tf), lambda i, f, be, nb: (be[i], 0, nf + fcol(i, f, nb))),
            pl.BlockSpec((1, tf, D), lambda i, f, be, nb: (be[i], fcol(i, f, nb), 0)),
        ],
        out_specs=pl.BlockSpec((MOE_TILE, D), lambda i, f, be, nb: (i, 0)),
        scratch_shapes=[pltpu.VMEM((MOE_TILE, D), F32)],
    )
    return pl.pallas_call(
        _moe_ffn_kernel,
        grid_spec=grid_spec,
        out_shape=jax.ShapeDtypeStruct((P, D), F32),
        compiler_params=_cparams(("arbitrary", "arbitrary")),
        name="moe_ffn",
    )(block_e, n_used, xg, slot_gate, w13, w13, w2)


def _router_kernel(x_ref, sc_ref, sh_ref, wr_ref, h_ref, e_ref, g_ref):
    h = _norm_mod(x_ref[0], sc_ref[0], sh_ref[0])
    h_ref[0] = h.astype(BF16)
    lane = lax.broadcasted_iota(jnp.int32, (1, LANES), 1)
    lanef = lane.astype(F32)
    lg = jnp.dot(h, wr_ref[...], precision=HIGHEST, preferred_element_type=F32)
    lg = jnp.where(lane < N_EXPERTS, lg, NEG_BIG)
    m1 = jnp.max(lg, axis=-1, keepdims=True)
    i1 = jnp.min(jnp.where(lg == m1, lanef, float(LANES)), axis=-1, keepdims=True)
    lg2 = jnp.where(lanef == i1, NEG_BIG, lg)
    m2 = jnp.max(lg2, axis=-1, keepdims=True)
    i2 = jnp.min(jnp.where(lg2 == m2, lanef, float(LANES)), axis=-1, keepdims=True)
    e = jnp.exp(m2 - m1)
    g1 = 1.0 / (1.0 + e)
    g2 = e / (1.0 + e)
    e_ref[0] = jnp.where(lane == 0, i1, jnp.where(lane == 1, i2, 0.0)).astype(jnp.int32)
    g_ref[0] = jnp.where(lane == 0, g1, jnp.where(lane == 1, g2, 0.0))


def _router(x, sc, sh, w_router):
    B, S, _ = x.shape
    return pl.pallas_call(
        _router_kernel,
        grid=(B, S // ROW_TILE),
        in_specs=[_row_spec(D), _vec_spec(), _vec_spec(), _full_spec((D, LANES))],
        out_specs=[_row_spec(D), _row_spec(LANES), _row_spec(LANES)],
        out_shape=[jax.ShapeDtypeStruct((B, S, D), BF16),
                   jax.ShapeDtypeStruct((B, S, LANES), jnp.int32),
                   jax.ShapeDtypeStruct((B, S, LANES), F32)],
        compiler_params=_cparams(("parallel", "parallel")),
        name="router",
    )(x, sc, sh, w_router)


def _combine_kernel(x_ref, g_ref, ya_ref, yb_ref, o_ref):
    o_ref[0] = x_ref[0] + g_ref[0] * (ya_ref[0] + yb_ref[0])


def _combine_norm_kernel(x_ref, g_ref, ya_ref, yb_ref, gf_ref, o_ref):
    x = x_ref[0] + g_ref[0] * (ya_ref[0] + yb_ref[0])
    ms = jnp.mean(x * x, axis=-1, keepdims=True)
    o_ref[0] = x * lax.rsqrt(ms + EPS) * gf_ref[...]


def _combine(x, g, ya, yb, norm_g=None):
    B, S, _ = x.shape
    in_specs = [_row_spec(D), _vec_spec(), _row_spec(D), _row_spec(D)]
    args = [x, g, ya, yb]
    body = _combine_kernel
    if norm_g is not None:
        in_specs.append(_full_spec((1, D)))
        args.append(norm_g.reshape(1, D))
        body = _combine_norm_kernel
    return pl.pallas_call(
        body,
        grid=(B, S // ROW_TILE),
        in_specs=in_specs,
        out_specs=_row_spec(D),
        out_shape=jax.ShapeDtypeStruct((B, S, D), F32),
        compiler_params=_cparams(("parallel", "parallel")),
        name="moe_combine",
    )(*args)


def _final_norm_kernel(x_ref, gf_ref, o_ref):
    x = x_ref[0]
    ms = jnp.mean(x * x, axis=-1, keepdims=True)
    o_ref[0] = x * lax.rsqrt(ms + EPS) * gf_ref[...]


def _final_norm(x, norm_g):
    B, S, _ = x.shape
    return pl.pallas_call(
        _final_norm_kernel,
        grid=(B, S // ROW_TILE),
        in_specs=[_row_spec(D), _full_spec((1, D))],
        out_specs=_row_spec(D),
        out_shape=jax.ShapeDtypeStruct((B, S, D), F32),
        compiler_params=_cparams(("parallel", "parallel")),
        name="final_norm",
    )(x, norm_g.reshape(1, D))


def _pad_cols(w, width):
    return jnp.pad(w, ((0, 0), (0, width - w.shape[1])))


def _gla_layer(x, sc, sh, g, w_in, w_gate2, b_gate, gn_g, w_out):
    n_main = 2 * GLA_DK + 2 * GLA_DV
    w_main = w_in[:, :n_main].astype(BF16)
    w_glr = _pad_cols(w_in[:, n_main:], LANES).astype(BF16)
    w_g2 = jnp.pad(w_gate2, ((0, LANES - GLA_RANK), (0, 0))).astype(BF16)
    qkvr, log_a = _gla_proj(x, sc, sh, w_main, w_glr, w_g2, b_gate.reshape(1, GLA_DK))
    y = _gla_scan(qkvr, log_a, gn_g)
    return _out_proj(y, w_out.astype(BF16), jnp.zeros((1, D), F32), x, g)


def _conv_layer(x, sc, sh, g, w_in, b_in, dw, dw_b, ln_g, ln_b, w_out, b_out):
    u = _conv_proj(x, sc, sh, w_in.astype(BF16), b_in.reshape(1, 2 * D))
    return _conv_post(u, dw, dw_b.reshape(1, D), ln_g.reshape(1, D), ln_b.reshape(1, D),
                      w_out.astype(BF16), b_out.reshape(1, D), x, g)


def _fox_layer(x, sc, sh, g, w_in, b_f, qn_g, kn_g, w_out):
    w_main = jnp.concatenate([w_in[:, :3 * D], w_in[:, 3 * D + FOX_HEADS:]], axis=1).astype(BF16)
    w_fl = _pad_cols(w_in[:, 3 * D:3 * D + FOX_HEADS], LANES).astype(BF16)
    b_fl = jnp.pad(b_f, (0, LANES - FOX_HEADS)).reshape(1, LANES)
    qkvo, lf = _fox_proj(x, sc, sh, w_main, w_fl, b_fl)
    cum = _fox_cum(lf)
    qa, ka = _fox_prep(qkvo, cum, qn_g, kn_g)
    y = _fox_flash(qa, ka, qkvo)
    return _out_proj(y, w_out.astype(BF16), jnp.zeros((1, D), F32), x, g)


def _moe_layer(x, sc, sh, g, w_router, w13, w2, norm_g):
    B, S, _ = x.shape
    N = B * S
    A = 2 * N
    h, e_pad, g_pad = _router(x, sc, sh, _pad_cols(w_router, LANES))
    e_flat = e_pad[:, :, :2].reshape(A)
    gates = g_pad[:, :, :2].reshape(A)

    onehot = (e_flat[:, None] == jnp.arange(N_EXPERTS, dtype=jnp.int32)[None, :]).astype(jnp.int32)
    csum = jnp.cumsum(onehot, axis=0)
    rank = jnp.sum(csum * onehot, axis=1) - 1
    counts = csum[-1]
    padded = (counts + MOE_TILE - 1) // MOE_TILE * MOE_TILE
    pad_end = jnp.cumsum(padded)
    pad_start = pad_end - padded
    dest = pad_start[e_flat] + rank
    n_blocks = A // MOE_TILE + N_EXPERTS
    P = n_blocks * MOE_TILE
    tok = jnp.arange(A, dtype=jnp.int32) // 2
    slot_tok = jnp.zeros((P,), jnp.int32).at[dest].set(tok)
    slot_gate = jnp.zeros((P,), F32).at[dest].set(gates)
    block_e = jnp.minimum(
        jnp.searchsorted(pad_end, jnp.arange(n_blocks, dtype=jnp.int32) * MOE_TILE, side='right'),
        N_EXPERTS - 1).astype(jnp.int32)
    n_used = (pad_end[-1] // MOE_TILE).astype(jnp.int32).reshape(1)

    xg = jnp.take(h.reshape(N, D), slot_tok, axis=0)
    yb = _moe_ffn(block_e, n_used, xg, slot_gate.reshape(P, 1), w13.astype(BF16), w2.astype(BF16))
    dest2 = dest.reshape(N, 2)
    ya = jnp.take(yb, dest2[:, 0], axis=0).reshape(B, S, D)
    yc = jnp.take(yb, dest2[:, 1], axis=0).reshape(B, S, D)
    return _combine(x, g, ya, yc, norm_g)


def kernel(x, c, ada_w, ada_b, gla_w_in, gla_w_gate2, gla_b_gate, gla_gn_g, gla_w_out, conv_w_in, conv_b_in, conv_dw, conv_dw_b, conv_ln_g, conv_ln_b, conv_w_out, conv_b_out, fox_w_in, fox_b_f, fox_qn_g, fox_kn_g, fox_w_out, ffn_w13, ffn_w2, moe_router, moe_w13, moe_w2, norm_f_g):
    depth = ada_w.shape[0]
    B = x.shape[0]
    mod = _ada_mod(c, ada_w, ada_b).reshape(depth, 6, B, 1, D)
    out = None
    for i in range(depth):
        sh1, sc1, g1, sh2, sc2, g2 = [mod[i, j] for j in range(6)]
        m, j = i % 3, i // 3
        if m == 0:
            x = _gla_layer(x, sc1, sh1, g1, gla_w_in[j], gla_w_gate2[j], gla_b_gate[j],
                           gla_gn_g[j], gla_w_out[j])
        elif m == 1:
            x = _conv_layer(x, sc1, sh1, g1, conv_w_in[j], conv_b_in[j], conv_dw[j], conv_dw_b[j],
                            conv_ln_g[j], conv_ln_b[j], conv_w_out[j], conv_b_out[j])
        else:
            x = _fox_layer(x, sc1, sh1, g1, fox_w_in[j], fox_b_f[j], fox_qn_g[j], fox_kn_g[j],
                           fox_w_out[j])
        last = i == depth - 1
        if i % 2 == 0:
            x = _dense_ffn(x, sc2, sh2, g2, ffn_w13[i // 2].astype(BF16), ffn_w2[i // 2].astype(BF16))
            if last:
                out = _final_norm(x, norm_f_g)
        else:
            x = _moe_layer(x, sc2, sh2, g2, moe_router[i // 2], moe_w13[i // 2], moe_w2[i // 2],
                           norm_f_g if last else None)
            if last:
                out = x
    return out
```

```python
import math

import jax
import jax.numpy as jnp
from jax import lax
from jax.experimental import pallas as pl
from jax.experimental.pallas import tpu as pltpu

F32 = jnp.float32
BF16 = jnp.bfloat16
HIGHEST = lax.Precision.HIGHEST

D = 1024
EPS = 1e-6
CHUNK = 64
GLA_HEADS = 4
GLA_DK = 512
GLA_DV = 1024
GLA_DKH = 128
GLA_DVH = 256
GLA_RANK = 16
GLA_TAU = 16.0
CONV_WIDTH = 31
CONV_HALO = 32
FOX_HEADS = 16
FOX_DH = 64
N_EXPERTS = 8
D_FF = 2816
D_FF_EXPERT = 3584
LANES = 128
SUBLANES = 8
NEG_BIG = -1e30
LOG2E = 1.4426950408889634
VMEM_LIMIT = 56 * 1024 * 1024

ROW_TILE = 512
MOE_TILE = 512
FF_SUB = 256
CAST_BLOCK_ELEMS = 1 << 20


def _cparams(sem):
    return pltpu.CompilerParams(dimension_semantics=sem, vmem_limit_bytes=VMEM_LIMIT)


def _dot(a, b):
    return jnp.dot(a, b, preferred_element_type=F32)


def _dot_nt(a, b):
    return lax.dot_general(a, b, (((1,), (1,)), ((), ())), preferred_element_type=F32)


def _dot_tn(a, b):
    return lax.dot_general(a, b, (((0,), (0,)), ((), ())), preferred_element_type=F32)


def _sigmoid(x):
    return 1.0 / (1.0 + jnp.exp(-x))


def _log_sigmoid(z):
    return jnp.minimum(z, 0.0) - jnp.log(1.0 + jnp.exp(-jnp.abs(z)))


def _norm_mod(x, sc, sh):
    ms = jnp.mean(x * x, axis=-1, keepdims=True)
    return x * lax.rsqrt(ms + EPS) * (1.0 + sc) + sh


def _split3(x):
    hi = x.astype(BF16).astype(F32)
    r1 = x - hi
    mid = r1.astype(BF16).astype(F32)
    lo = (r1 - mid).astype(BF16).astype(F32)
    return hi, mid, lo


def _tri_cumsum(tri, x):
    hi, mid, lo = _split3(x)
    return _dot(tri, hi.astype(BF16)) + _dot(tri, mid.astype(BF16)) + _dot(tri, lo.astype(BF16))


def _lower_tri(n):
    r = lax.broadcasted_iota(jnp.int32, (n, n), 0)
    c = lax.broadcasted_iota(jnp.int32, (n, n), 1)
    return jnp.where(r >= c, 1.0, 0.0).astype(BF16)


def _cast_kernel(x_ref, o_ref):
    o_ref[...] = x_ref[...].astype(BF16)


def _cast_bf16(w):
    cols = w.shape[-1]
    rows = math.prod(w.shape[:-1])
    block = math.gcd(rows, 1 << (CAST_BLOCK_ELEMS // cols).bit_length() - 1)
    spec = pl.BlockSpec((block, cols), lambda i: (i, 0))
    out = pl.pallas_call(
        _cast_kernel,
        grid=(rows // block,),
        in_specs=[spec],
        out_specs=spec,
        out_shape=jax.ShapeDtypeStruct((rows, cols), BF16),
        compiler_params=_cparams(("parallel",)),
        name="cast_bf16",
    )(w.reshape(rows, cols))
    return out.reshape(w.shape)


def _ada_kernel(c_ref, w_ref, b_ref, o_ref):
    c = c_ref[...]
    cond = c * _sigmoid(c)
    o_ref[0, 0] = jnp.dot(cond, w_ref[0], precision=HIGHEST,
                          preferred_element_type=F32) + b_ref[0, 0]


def _ada_mod(c, ada_w, ada_b):
    depth = ada_w.shape[0]
    B = c.shape[0]
    return pl.pallas_call(
        _ada_kernel,
        grid=(depth, 6),
        in_specs=[
            pl.BlockSpec((B, D), lambda l, j: (0, 0)),
            pl.BlockSpec((1, D, D), lambda l, j: (l, 0, j)),
            pl.BlockSpec((1, 1, 1, D), lambda l, j: (l, j, 0, 0)),
        ],
        out_specs=pl.BlockSpec((1, 1, B, D), lambda l, j: (l, j, 0, 0)),
        out_shape=jax.ShapeDtypeStruct((depth, 6, B, D), F32),
        compiler_params=_cparams(("arbitrary", "arbitrary")),
        name="ada_mod",
    )(c, ada_w, ada_b.reshape(depth, 6, 1, D))


def _row_spec(width, col=0):
    return pl.BlockSpec((1, ROW_TILE, width), lambda b, i: (b, i, col))


def _vec_spec(width=D):
    return pl.BlockSpec((1, 1, width), lambda b, i: (b, 0, 0))


def _full_spec(shape):
    return pl.BlockSpec(shape, lambda b, i: (0,) * len(shape))


def _gla_proj_kernel(x_ref, sc_ref, sh_ref, w_ref, wg_ref, w2_ref, bg_ref, o_ref, la_ref):
    h = _norm_mod(x_ref[0], sc_ref[0], sh_ref[0]).astype(BF16)
    n_out = o_ref.shape[2]
    for c in range(0, n_out, 512):
        o_ref[0, :, c:c + 512] = _dot(h, w_ref[:, c:c + 512]).astype(BF16)
    glr = _dot(h, wg_ref[...]).astype(BF16)
    z = _dot(glr, w2_ref[...]) + bg_ref[...]
    la_ref[0] = _log_sigmoid(z) * (1.0 / GLA_TAU)


def _gla_proj(x, sc, sh, w_main, w_glr, w_g2, b_gate):
    B, S, _ = x.shape
    n_out = w_main.shape[1]
    return pl.pallas_call(
        _gla_proj_kernel,
        grid=(B, S // ROW_TILE),
        in_specs=[_row_spec(D), _vec_spec(), _vec_spec(),
                  _full_spec((D, n_out)), _full_spec((D, LANES)),
                  _full_spec((LANES, GLA_DK)), _full_spec((1, GLA_DK))],
        out_specs=[_row_spec(n_out), _row_spec(GLA_DK)],
        out_shape=[jax.ShapeDtypeStruct((B, S, n_out), BF16),
                   jax.ShapeDtypeStruct((B, S, GLA_DK), F32)],
        compiler_params=_cparams(("parallel", "parallel")),
        name="gla_proj",
    )(x, sc, sh, w_main, w_glr, w_g2, b_gate)


def _conv_proj_kernel(x_ref, sc_ref, sh_ref, w_ref, b_ref, o_ref):
    h = _norm_mod(x_ref[0], sc_ref[0], sh_ref[0]).astype(BF16)
    for c in range(0, D, 512):
        a = _dot(h, w_ref[:, c:c + 512]) + b_ref[:, c:c + 512]
        g = _dot(h, w_ref[:, D + c:D + c + 512]) + b_ref[:, D + c:D + c + 512]
        o_ref[0, :, c:c + 512] = (a * _sigmoid(g)).astype(BF16)


def _conv_proj(x, sc, sh, w_in, b_in):
    B, S, _ = x.shape
    return pl.pallas_call(
        _conv_proj_kernel,
        grid=(B, S // ROW_TILE),
        in_specs=[_row_spec(D), _vec_spec(), _vec_spec(),
                  _full_spec((D, 2 * D)), _full_spec((1, 2 * D))],
        out_specs=_row_spec(D),
        out_shape=jax.ShapeDtypeStruct((B, S, D), BF16),
        compiler_params=_cparams(("parallel", "parallel")),
        name="conv_proj",
    )(x, sc, sh, w_in, b_in)


def _fox_proj_kernel(x_ref, sc_ref, sh_ref, w_ref, wf_ref, bf_ref, o_ref, lf_ref):
    h = _norm_mod(x_ref[0], sc_ref[0], sh_ref[0]).astype(BF16)
    n_out = o_ref.shape[2]
    for c in range(0, n_out, 512):
        o_ref[0, :, c:c + 512] = _dot(h, w_ref[:, c:c + 512]).astype(BF16)
    lf_ref[0] = _log_sigmoid(_dot(h, wf_ref[...]) + bf_ref[...])


def _fox_proj(x, sc, sh, w_main, w_fl, b_f):
    B, S, _ = x.shape
    n_out = w_main.shape[1]
    return pl.pallas_call(
        _fox_proj_kernel,
        grid=(B, S // ROW_TILE),
        in_specs=[_row_spec(D), _vec_spec(), _vec_spec(),
                  _full_spec((D, n_out)), _full_spec((D, LANES)), _full_spec((1, LANES))],
        out_specs=[_row_spec(n_out), _row_spec(LANES)],
        out_shape=[jax.ShapeDtypeStruct((B, S, n_out), BF16),
                   jax.ShapeDtypeStruct((B, S, LANES), F32)],
        compiler_params=_cparams(("parallel", "parallel")),
        name="fox_proj",
    )(x, sc, sh, w_main, w_fl, b_f)


def _out_proj_kernel(y_ref, w_ref, b_ref, x_ref, g_ref, o_ref):
    for c in range(0, D, 512):
        y = _dot(y_ref[0], w_ref[:, c:c + 512]) + b_ref[:, c:c + 512]
        o_ref[0, :, c:c + 512] = x_ref[0, :, c:c + 512] + g_ref[0][:, c:c + 512] * y


def _out_proj(y, w, b, x, g):
    B, S, _ = x.shape
    return pl.pallas_call(
        _out_proj_kernel,
        grid=(B, S // ROW_TILE),
        in_specs=[_row_spec(D), _full_spec((D, D)), _full_spec((1, D)), _row_spec(D), _vec_spec()],
        out_specs=_row_spec(D),
        out_shape=jax.ShapeDtypeStruct((B, S, D), F32),
        compiler_params=_cparams(("parallel", "parallel")),
        name="out_proj",
    )(y, w, b, x, g)


def _gla_scan_kernel(q_ref, k_ref, v_ref, r_ref, la_ref, gn_ref, o_ref, st_ref):
    @pl.when(pl.program_id(1) == 0)
    def _():
        st_ref[...] = jnp.zeros_like(st_ref)

    tri = _lower_tri(CHUNK)
    scale = GLA_DKH ** -0.5
    states = [st_ref[h] for h in range(GLA_HEADS)]
    for c in range(ROW_TILE // CHUNK):
        rows = slice(c * CHUNK, (c + 1) * CHUNK)
        cum = _tri_cumsum(tri, la_ref[0, rows, :])
        tot = cum[CHUNK - 1:CHUNK, :]
        kd = (k_ref[0, rows, :].astype(F32) * jnp.exp(tot - cum)).astype(BF16)
        decay = jnp.exp(tot)
        for h in range(GLA_HEADS):
            ks = slice(h * GLA_DKH, (h + 1) * GLA_DKH)
            vs = slice(h * GLA_DVH, (h + 1) * GLA_DVH)
            st = states[h] * decay[:, ks] + _dot_tn(v_ref[0, rows, vs], kd[:, ks])
            states[h] = st
            o = _dot_nt(q_ref[0, rows, ks], st.astype(BF16)) * scale
            ms = jnp.mean(o * o, axis=-1, keepdims=True)
            on = o * lax.rsqrt(ms + EPS) * gn_ref[:, vs]
            r = r_ref[0, rows, vs].astype(F32)
            o_ref[0, rows, vs] = (on * (r * _sigmoid(r))).astype(BF16)
    for h in range(GLA_HEADS):
        st_ref[h] = states[h]


def _gla_scan(qkvr, log_a, gn_g):
    B, S, _ = qkvr.shape
    return pl.pallas_call(
        _gla_scan_kernel,
        grid=(B, S // ROW_TILE),
        in_specs=[_row_spec(GLA_DK, 0), _row_spec(GLA_DK, 1), _row_spec(GLA_DV, 1), _row_spec(GLA_DV, 2),
                  _row_spec(GLA_DK, 0), _full_spec((1, GLA_DV))],
        out_specs=_row_spec(GLA_DV),
        out_shape=jax.ShapeDtypeStruct((B, S, GLA_DV), BF16),
        scratch_shapes=[pltpu.VMEM((GLA_HEADS, GLA_DVH, GLA_DKH), F32)],
        compiler_params=_cparams(("parallel", "arbitrary")),
        name="gla_scan",
    )(qkvr, qkvr, qkvr, qkvr, log_a, gn_g.reshape(1, GLA_DV))


CONV_ROWS = 64
CONV_COLS = 256


def _conv_post_kernel(uc_ref, up_ref, dw_ref, dwb_ref, lng_ref, lnb_ref, w_ref, bo_ref,
                      x_ref, g_ref, o_ref, ext_ref, sh_ref, cv_ref):
    first = pl.program_id(1) == 0
    halo = up_ref[0].astype(F32)
    ext_ref[0:CONV_HALO, :] = jnp.where(first, 0.0, halo)
    ext_ref[CONV_HALO:, :] = uc_ref[0].astype(F32)
    lead = CONV_HALO - (CONV_WIDTH - 1)
    for j in range(1, SUBLANES):
        sh_ref[j - 1] = ext_ref[pl.ds(j, sh_ref.shape[1]), :]

    for c in range(0, D, CONV_COLS):
        def row_group(r, carry, c=c):
            r0 = pl.multiple_of(r * CONV_ROWS, CONV_ROWS)
            acc = jnp.zeros((CONV_ROWS, CONV_COLS), F32) + dwb_ref[:, c:c + CONV_COLS]
            for w in range(CONV_WIDTH):
                j = (lead + w) % SUBLANES
                rows = pl.ds(pl.multiple_of(r0 + (lead + w - j), SUBLANES), CONV_ROWS)
                src = ext_ref if j == 0 else sh_ref.at[j - 1]
                acc = acc + dw_ref[w:w + 1, c:c + CONV_COLS] * src[rows, c:c + CONV_COLS]
            cv_ref[pl.ds(r0, CONV_ROWS), c:c + CONV_COLS] = acc
            return carry
        lax.fori_loop(0, ROW_TILE // CONV_ROWS, row_group, 0)

    u = cv_ref[...]
    mu = jnp.mean(u, axis=-1, keepdims=True)
    var = jnp.mean(jnp.square(u - mu), axis=-1, keepdims=True)
    un = (u - mu) * lax.rsqrt(var + EPS) * lng_ref[...] + lnb_ref[...]
    y = (un * _sigmoid(un)).astype(BF16)
    for c in range(0, D, 512):
        z = _dot(y, w_ref[:, c:c + 512]) + bo_ref[:, c:c + 512]
        o_ref[0, :, c:c + 512] = x_ref[0, :, c:c + 512] + g_ref[0][:, c:c + 512] * z


def _conv_post(u, dw, dw_b, ln_g, ln_b, w_out, b_out, x, g):
    B, S, _ = x.shape
    halo_blocks = ROW_TILE // CONV_HALO
    return pl.pallas_call(
        _conv_post_kernel,
        grid=(B, S // ROW_TILE),
        in_specs=[
            _row_spec(D),
            pl.BlockSpec((1, CONV_HALO, D), lambda b, i: (b, jnp.maximum(i * halo_blocks - 1, 0), 0)),
            _full_spec((CONV_WIDTH, D)), _full_spec((1, D)), _full_spec((1, D)), _full_spec((1, D)),
            _full_spec((D, D)), _full_spec((1, D)), _row_spec(D), _vec_spec(),
        ],
        out_specs=_row_spec(D),
        out_shape=jax.ShapeDtypeStruct((B, S, D), F32),
        scratch_shapes=[pltpu.VMEM((ROW_TILE + CONV_HALO, D), F32),
                        pltpu.VMEM((SUBLANES - 1, ROW_TILE + CONV_HALO - SUBLANES, D), F32),
                        pltpu.VMEM((ROW_TILE, D), F32)],
        compiler_params=_cparams(("parallel", "parallel")),
        name="conv_post",
    )(u, u, dw, dw_b, ln_g, ln_b, w_out, b_out, x, g)


CUM_TILE = 256
FOX_PART_COPIES = 3


def _fox_cum_kernel(lf_ref, o_ref, carry_ref):
    @pl.when(pl.program_id(1) == 0)
    def _():
        carry_ref[...] = jnp.zeros_like(carry_ref)

    cum = _tri_cumsum(_lower_tri(CUM_TILE), lf_ref[0]) + carry_ref[...]
    o_ref[0] = cum
    carry_ref[...] = cum[CUM_TILE - 1:CUM_TILE, :]


def _fox_cum(lf):
    B, S, _ = lf.shape
    spec = pl.BlockSpec((1, CUM_TILE, LANES), lambda b, i: (b, i, 0))
    return pl.pallas_call(
        _fox_cum_kernel,
        grid=(B, S // CUM_TILE),
        in_specs=[spec],
        out_specs=spec,
        out_shape=jax.ShapeDtypeStruct((B, S, LANES), F32),
        scratch_shapes=[pltpu.VMEM((1, LANES), F32)],
        compiler_params=_cparams(("parallel", "arbitrary")),
        name="fox_cum",
    )(lf)


def _fox_prep_kernel(q_ref, k_ref, cum_ref, qg_ref, kg_ref, qa_ref, ka_ref):
    pair = pl.program_id(1)
    lane = lax.broadcasted_iota(jnp.int32, (1, LANES), 1)
    r2 = lax.broadcasted_iota(jnp.int32, (2 * LANES, LANES), 0)
    c2 = lax.broadcasted_iota(jnp.int32, (2 * LANES, LANES), 1)
    same_head = ((r2 & (LANES - 1)) >> 6) == (c2 >> 6)
    head_sum = jnp.where(same_head, 1.0, 0.0).astype(BF16)

    def head_norm(t_ref, g_ref, scale):
        t = t_ref[0].astype(F32)
        sq = t * t
        hi = sq.astype(BF16)
        mid = (sq - hi.astype(F32)).astype(BF16)
        ss = _dot(jnp.concatenate([hi, mid], axis=1), head_sum)
        return (t * lax.rsqrt(ss * (1.0 / FOX_DH) + EPS) * (g_ref[...] * scale)).astype(BF16)

    qb = head_norm(q_ref, qg_ref, LOG2E * FOX_DH ** -0.5)
    kb = head_norm(k_ref, kg_ref, 1.0)
    c_hi, c_mid, c_lo = _split3(cum_ref[0] * LOG2E)
    parts = jnp.where(lane < FOX_HEADS, c_hi,
            jnp.where(lane < 2 * FOX_HEADS, c_mid,
            jnp.where(lane < 3 * FOX_HEADS, c_lo, 0.0))).astype(BF16)
    lhs_q = jnp.concatenate([qb, parts], axis=1)
    lhs_k = jnp.concatenate([kb, parts], axis=1)
    d = FOX_DH
    for hh in range(2):
        h = 2 * pair + hh
        part_row = LANES + h
        src_q = jnp.where(lane < d, lane + hh * d,
                jnp.where(lane < d + 3, part_row + FOX_HEADS * (lane - d), -1))
        src_k = jnp.where(lane < d, lane + hh * d,
                jnp.where((lane >= d + 3) & (lane < d + 6), part_row + FOX_HEADS * (lane - d - 3), -1))
        wq = jnp.where(r2 == src_q, 1.0, 0.0).astype(BF16)
        wk = jnp.where(r2 == src_k, jnp.where(lane < d, 1.0, -1.0), 0.0).astype(BF16)
        ones_q = jnp.where((lane >= d + 3) & (lane < d + 6), 1.0, 0.0)
        ones_k = jnp.where((lane >= d) & (lane < d + 3), 1.0, 0.0)
        qa_ref[0, hh] = (_dot(lhs_q, wq) + ones_q).astype(BF16)
        ka_ref[0, hh] = (_dot(lhs_k, wk) + ones_k).astype(BF16)


def _fox_prep(qkvo, cum, qn_g, kn_g):
    B, S, _ = qkvo.shape
    pairs = FOX_HEADS // 2
    g2 = lambda g: jnp.concatenate([g, g]).reshape(1, LANES)
    aug = pl.BlockSpec((1, 2, ROW_TILE, LANES), lambda b, p, i: (b, p, i, 0))
    return pl.pallas_call(
        _fox_prep_kernel,
        grid=(B, pairs, S // ROW_TILE),
        in_specs=[
            pl.BlockSpec((1, ROW_TILE, LANES), lambda b, p, i: (b, i, p)),
            pl.BlockSpec((1, ROW_TILE, LANES), lambda b, p, i: (b, i, pairs + p)),
            pl.BlockSpec((1, ROW_TILE, LANES), lambda b, p, i: (b, i, 0)),
            pl.BlockSpec((1, LANES), lambda b, p, i: (0, 0)),
            pl.BlockSpec((1, LANES), lambda b, p, i: (0, 0)),
        ],
        out_specs=[aug, aug],
        out_shape=[jax.ShapeDtypeStruct((B, FOX_HEADS, S, LANES), BF16)] * 2,
        compiler_params=_cparams(("parallel", "parallel", "parallel")),
        name="fox_prep",
    )(qkvo, qkvo, cum, g2(qn_g), g2(kn_g))


ATT_TILE = 512


def _fox_flash_kernel(qt_ref, kt_ref, qa_ref, ka_ref, v_ref, og_ref, o_ref,
                      vx_ref, cap_ref, s0_ref, s1_ref, p0_ref, p1_ref, al0_ref, al1_ref, m_ref, acc_ref):
    T = ATT_TILE
    n_items = qt_ref.shape[0]
    lane = lax.broadcasted_iota(jnp.int32, (1, LANES), 1)
    s_refs, p_refs, al_refs = (s0_ref, s1_ref), (p0_ref, p1_ref), (al0_ref, al1_ref)

    vx_ref[:, :LANES] = v_ref[0]
    ones_col = jnp.where(lane == 0, 1.0, 0.0).astype(BF16)
    vx_ref[:, LANES:] = jnp.broadcast_to(ones_col, (vx_ref.shape[0], LANES))
    causal = (lax.broadcasted_iota(jnp.int32, (T, T), 1) <= lax.broadcasted_iota(jnp.int32, (T, T), 0))
    cap_ref[...] = jnp.where(causal, -NEG_BIG, NEG_BIG)
    for slot in range(2):
        s_refs[slot][...] = jnp.zeros_like(s_refs[slot])
        p_refs[slot][...] = jnp.zeros_like(p_refs[slot])
        al_refs[slot][...] = jnp.ones_like(al_refs[slot])
    m_ref[...] = jnp.full(m_ref.shape, NEG_BIG, F32)
    acc_ref[...] = jnp.zeros_like(acc_ref)

    def block(i):
        return pl.ds(pl.multiple_of(i * T, T), T)

    def tick(t, slot):
        it = jnp.minimum(t, n_items - 1)
        q1, k1 = qt_ref[it], kt_ref[it]
        for hh in range(2):
            s_refs[slot][hh] = _dot_nt(qa_ref[0, hh, block(q1), :], ka_ref[0, hh, block(k1), :])
        it = jnp.clip(t - 1, 0, n_items - 1)
        live = (t >= 1) & (t <= n_items)
        fresh = kt_ref[it] == 0
        for hh in range(2):
            s = s_refs[1 - slot][hh]
            m_old = jnp.where(fresh, NEG_BIG, m_ref[hh])
            m_new = jnp.maximum(m_old, jnp.max(s, axis=-1, keepdims=True))
            m_ref[hh] = jnp.where(live, m_new, m_ref[hh])
            al_refs[1 - slot][hh] = jnp.where(live, jnp.exp2(m_old - m_new), 1.0)
            m_use = jnp.where(live, m_new, -NEG_BIG)
            p_refs[1 - slot][hh] = jnp.exp2(s - pltpu.repeat(m_use, T // LANES, axis=1)).astype(BF16)
        it = jnp.clip(t - 2, 0, n_items - 1)
        q3, k3 = qt_ref[it], kt_ref[it]
        for hh in range(2):
            alpha = pltpu.repeat(al_refs[slot][hh], 2, axis=1)
            acc_ref[hh] = alpha * acc_ref[hh] + _dot(p_refs[slot][hh], vx_ref[block(k3), :])

        @pl.when(q1 == k1)
        def _():
            for hh in range(2):
                s_refs[slot][hh] = jnp.minimum(s_refs[slot][hh], cap_ref[...])

        @pl.when((t >= 2) & (t <= n_items + 1) & (q3 == k3))
        def _():
            heads = [acc_ref[hh, :, :LANES] / acc_ref[hh, :, LANES:LANES + 1] for hh in range(2)]
            o = jnp.where(lane < FOX_DH, heads[0], heads[1])
            gate = _sigmoid(og_ref[0, block(q3), :].astype(F32))
            o_ref[0, block(q3), :] = (o * gate).astype(BF16)

    def two_ticks(i, carry):
        tick(2 * i, 0)
        tick(2 * i + 1, 1)
        return carry

    lax.fori_loop(0, (n_items + 3) // 2, two_ticks, 0)


def _fox_flash(qa, ka, qkvo):
    B, _, S, _ = qa.shape
    T = ATT_TILE
    pairs = FOX_HEADS // 2
    v_off = 2 * pairs
    og_off = 3 * pairs
    items = [(qi, kb) for qi in range(S // T) for kb in range(qi + 1)]
    q_tab = jnp.asarray([qi for qi, _ in items], jnp.int32)
    k_tab = jnp.asarray([kb for _, kb in items], jnp.int32)
    grid_spec = pltpu.PrefetchScalarGridSpec(
        num_scalar_prefetch=2,
        grid=(B, pairs),
        in_specs=[
            pl.BlockSpec((1, 2, S, LANES), lambda b, p, qt, kt: (b, p, 0, 0)),
            pl.BlockSpec((1, 2, S, LANES), lambda b, p, qt, kt: (b, p, 0, 0)),
            pl.BlockSpec((1, S, LANES), lambda b, p, qt, kt: (b, 0, v_off + p)),
            pl.BlockSpec((1, S, LANES), lambda b, p, qt, kt: (b, 0, og_off + p)),
        ],
        out_specs=pl.BlockSpec((1, S, LANES), lambda b, p, qt, kt: (b, 0, p)),
        scratch_shapes=[
            pltpu.VMEM((S, 2 * LANES), BF16),
            pltpu.VMEM((T, T), F32),
            pltpu.VMEM((2, T, T), F32), pltpu.VMEM((2, T, T), F32),
            pltpu.VMEM((2, T, T), BF16), pltpu.VMEM((2, T, T), BF16),
            pltpu.VMEM((2, T, LANES), F32), pltpu.VMEM((2, T, LANES), F32),
            pltpu.VMEM((2, T, LANES), F32),
            pltpu.VMEM((2, T, 2 * LANES), F32),
        ],
    )
    return pl.pallas_call(
        _fox_flash_kernel,
        grid_spec=grid_spec,
        out_shape=jax.ShapeDtypeStruct((B, S, D), BF16),
        compiler_params=_cparams(("parallel", "parallel")),
        name="fox_flash",
    )(q_tab, k_tab, qa, ka, qkvo, qkvo)


def _swiglu_acc(h, w1_ref, w3_ref, w2_ref, acc_ref):
    tf = w2_ref.shape[0]
    for c in range(0, tf, FF_SUB):
        e = min(c + FF_SUB, tf)
        a = _dot(h, w1_ref[:, c:e])
        b = _dot(h, w3_ref[:, c:e])
        act = (a * _sigmoid(a) * b).astype(BF16)
        acc_ref[...] += _dot(act, w2_ref[c:e, :])


def _dense_ffn_kernel(x_ref, sc_ref, sh_ref, g_ref, w1_ref, w3_ref, w2_ref, o_ref, h_ref, acc_ref):
    f = pl.program_id(2)

    @pl.when(f == 0)
    def _():
        h_ref[...] = _norm_mod(x_ref[0], sc_ref[0], sh_ref[0]).astype(BF16)
        acc_ref[...] = jnp.zeros_like(acc_ref)

    _swiglu_acc(h_ref[...], w1_ref.at[0], w3_ref.at[0], w2_ref.at[0], acc_ref)

    @pl.when(f == pl.num_programs(2) - 1)
    def _():
        o_ref[0] = x_ref[0] + g_ref[0] * acc_ref[...]


DENSE_FF_TILE = 1408


def _dense_ffn(x, sc, sh, g, w13, w2, layer):
    B, S, _ = x.shape
    tf = DENSE_FF_TILE
    nf = D_FF // tf
    row = pl.BlockSpec((1, ROW_TILE, D), lambda b, i, f: (b, i, 0))
    vec = pl.BlockSpec((1, 1, D), lambda b, i, f: (b, 0, 0))
    return pl.pallas_call(
        _dense_ffn_kernel,
        grid=(B, S // ROW_TILE, nf),
        in_specs=[row, vec, vec, vec,
                  pl.BlockSpec((1, D, tf), lambda b, i, f: (layer, 0, f)),
                  pl.BlockSpec((1, D, tf), lambda b, i, f: (layer, 0, nf + f)),
                  pl.BlockSpec((1, tf, D), lambda b, i, f: (layer, f, 0))],
        out_specs=row,
        out_shape=jax.ShapeDtypeStruct((B, S, D), F32),
        scratch_shapes=[pltpu.VMEM((ROW_TILE, D), BF16), pltpu.VMEM((ROW_TILE, D), F32)],
        compiler_params=_cparams(("parallel", "parallel", "arbitrary")),
        name="dense_ffn",
    )(x, sc, sh, g, w13, w13, w2)


def _moe_ffn_kernel(be_ref, nb_ref, x_ref, w1_ref, w3_ref, w2_ref, o_ref):
    i = pl.program_id(0)

    @pl.when(pl.program_id(1) == 0)
    def _():
        o_ref[...] = jnp.zeros_like(o_ref)

    @pl.when(i < nb_ref[0])
    def _():
        _swiglu_acc(x_ref[...], w1_ref.at[0, 0], w3_ref.at[0, 0], w2_ref.at[0, 0], o_ref)


MOE_FF_TILE = 1792


def _moe_ffn(block_e, n_used, xg, w13, w2, layer):
    P = xg.shape[0]
    tf = MOE_FF_TILE
    nf = D_FF_EXPERT // tf
    last = nf - 1

    def held(i, nb):
        return jnp.minimum(i, nb[0] - 1)

    def fcol(i, f, nb):
        return jnp.where(i < nb[0], f, last)

    grid_spec = pltpu.PrefetchScalarGridSpec(
        num_scalar_prefetch=2,
        grid=(P // MOE_TILE, nf),
        in_specs=[
            pl.BlockSpec((MOE_TILE, D), lambda i, f, be, nb: (held(i, nb), 0)),
            pl.BlockSpec((1, 1, D, tf), lambda i, f, be, nb: (layer, be[i], 0, fcol(i, f, nb))),
            pl.BlockSpec((1, 1, D, tf), lambda i, f, be, nb: (layer, be[i], 0, nf + fcol(i, f, nb))),
            pl.BlockSpec((1, 1, tf, D), lambda i, f, be, nb: (layer, be[i], fcol(i, f, nb), 0)),
        ],
        out_specs=pl.BlockSpec((MOE_TILE, D), lambda i, f, be, nb: (i, 0)),
    )
    return pl.pallas_call(
        _moe_ffn_kernel,
        grid_spec=grid_spec,
        out_shape=jax.ShapeDtypeStruct((P, D), F32),
        compiler_params=_cparams(("arbitrary", "arbitrary")),
        name="moe_ffn",
    )(block_e, n_used, xg, w13, w13, w2)


def _router_kernel(x_ref, sc_ref, sh_ref, wr_ref, h_ref, e_ref, g_ref):
    h = _norm_mod(x_ref[0], sc_ref[0], sh_ref[0])
    h_ref[0] = h.astype(BF16)
    lane = lax.broadcasted_iota(jnp.int32, (1, LANES), 1)
    lanef = lane.astype(F32)
    lg = jnp.dot(h, wr_ref[...], precision=HIGHEST, preferred_element_type=F32)
    lg = jnp.where(lane < N_EXPERTS, lg, NEG_BIG)
    m1 = jnp.max(lg, axis=-1, keepdims=True)
    i1 = jnp.min(jnp.where(lg == m1, lanef, float(LANES)), axis=-1, keepdims=True)
    lg2 = jnp.where(lanef == i1, NEG_BIG, lg)
    m2 = jnp.max(lg2, axis=-1, keepdims=True)
    i2 = jnp.min(jnp.where(lg2 == m2, lanef, float(LANES)), axis=-1, keepdims=True)
    e = jnp.exp(m2 - m1)
    g1 = 1.0 / (1.0 + e)
    g2 = e / (1.0 + e)
    e_ref[0] = jnp.where(lane == 0, i1, jnp.where(lane == 1, i2, 0.0)).astype(jnp.int32)
    g_ref[0] = jnp.where(lane == 0, g1, jnp.where(lane == 1, g2, 0.0))


def _router(x, sc, sh, w_router):
    B, S, _ = x.shape
    return pl.pallas_call(
        _router_kernel,
        grid=(B, S // ROW_TILE),
        in_specs=[_row_spec(D), _vec_spec(), _vec_spec(), _full_spec((D, LANES))],
        out_specs=[_row_spec(D), _row_spec(LANES), _row_spec(LANES)],
        out_shape=[jax.ShapeDtypeStruct((B, S, D), BF16),
                   jax.ShapeDtypeStruct((B, S, LANES), jnp.int32),
                   jax.ShapeDtypeStruct((B, S, LANES), F32)],
        compiler_params=_cparams(("parallel", "parallel")),
        name="router",
    )(x, sc, sh, w_router)


def _combined(x_ref, g_ref, gate_ref, ya_ref, yb_ref):
    gate = gate_ref[0]
    y = ya_ref[0] * gate[:, 0:1] + yb_ref[0] * gate[:, 1:2]
    return x_ref[0] + g_ref[0] * y


def _combine_kernel(x_ref, g_ref, gate_ref, ya_ref, yb_ref, o_ref):
    o_ref[0] = _combined(x_ref, g_ref, gate_ref, ya_ref, yb_ref)


def _combine_norm_kernel(x_ref, g_ref, gate_ref, ya_ref, yb_ref, gf_ref, o_ref):
    x = _combined(x_ref, g_ref, gate_ref, ya_ref, yb_ref)
    ms = jnp.mean(x * x, axis=-1, keepdims=True)
    o_ref[0] = x * lax.rsqrt(ms + EPS) * gf_ref[...]


def _combine(x, g, gates, ya, yb, norm_g=None):
    B, S, _ = x.shape
    in_specs = [_row_spec(D), _vec_spec(), _row_spec(LANES), _row_spec(D), _row_spec(D)]
    args = [x, g, gates, ya, yb]
    body = _combine_kernel
    if norm_g is not None:
        in_specs.append(_full_spec((1, D)))
        args.append(norm_g.reshape(1, D))
        body = _combine_norm_kernel
    return pl.pallas_call(
        body,
        grid=(B, S // ROW_TILE),
        in_specs=in_specs,
        out_specs=_row_spec(D),
        out_shape=jax.ShapeDtypeStruct((B, S, D), F32),
        compiler_params=_cparams(("parallel", "parallel")),
        name="moe_combine",
    )(*args)


def _final_norm_kernel(x_ref, gf_ref, o_ref):
    x = x_ref[0]
    ms = jnp.mean(x * x, axis=-1, keepdims=True)
    o_ref[0] = x * lax.rsqrt(ms + EPS) * gf_ref[...]


def _final_norm(x, norm_g):
    B, S, _ = x.shape
    return pl.pallas_call(
        _final_norm_kernel,
        grid=(B, S // ROW_TILE),
        in_specs=[_row_spec(D), _full_spec((1, D))],
        out_specs=_row_spec(D),
        out_shape=jax.ShapeDtypeStruct((B, S, D), F32),
        compiler_params=_cparams(("parallel", "parallel")),
        name="final_norm",
    )(x, norm_g.reshape(1, D))


def _pad_cols(w, width):
    return jnp.pad(w, ((0, 0), (0, width - w.shape[1])))


def _gla_layer(x, sc, sh, g, w_in, w_gate2, b_gate, gn_g, w_out):
    n_main = 2 * GLA_DK + 2 * GLA_DV
    w_main = w_in[:, :n_main].astype(BF16)
    w_glr = _pad_cols(w_in[:, n_main:], LANES).astype(BF16)
    w_g2 = jnp.pad(w_gate2, ((0, LANES - GLA_RANK), (0, 0))).astype(BF16)
    qkvr, log_a = _gla_proj(x, sc, sh, w_main, w_glr, w_g2, b_gate.reshape(1, GLA_DK))
    y = _gla_scan(qkvr, log_a, gn_g)
    return _out_proj(y, w_out.astype(BF16), jnp.zeros((1, D), F32), x, g)


def _conv_layer(x, sc, sh, g, w_in, b_in, dw, dw_b, ln_g, ln_b, w_out, b_out):
    u = _conv_proj(x, sc, sh, w_in.astype(BF16), b_in.reshape(1, 2 * D))
    return _conv_post(u, dw, dw_b.reshape(1, D), ln_g.reshape(1, D), ln_b.reshape(1, D),
                      w_out.astype(BF16), b_out.reshape(1, D), x, g)


def _fox_layer(x, sc, sh, g, w_in, b_f, qn_g, kn_g, w_out):
    w_main = jnp.concatenate([w_in[:, :3 * D], w_in[:, 3 * D + FOX_HEADS:]], axis=1).astype(BF16)
    w_fl = _pad_cols(jnp.tile(w_in[:, 3 * D:3 * D + FOX_HEADS], (1, FOX_PART_COPIES)), LANES).astype(BF16)
    b_fl = _pad_cols(jnp.tile(b_f.reshape(1, FOX_HEADS), (1, FOX_PART_COPIES)), LANES)
    qkvo, lf = _fox_proj(x, sc, sh, w_main, w_fl, b_fl)
    cum = _fox_cum(lf)
    qa, ka = _fox_prep(qkvo, cum, qn_g, kn_g)
    y = _fox_flash(qa, ka, qkvo)
    return _out_proj(y, w_out.astype(BF16), jnp.zeros((1, D), F32), x, g)


def _moe_layer(x, sc, sh, g, w_router, w13, w2, layer, norm_g):
    B, S, _ = x.shape
    N = B * S
    A = 2 * N
    h, e_pad, gates = _router(x, sc, sh, _pad_cols(w_router, LANES))
    e_flat = e_pad[:, :, :2].reshape(A)

    onehot = (e_flat[:, None] == jnp.arange(N_EXPERTS, dtype=jnp.int32)[None, :]).astype(jnp.int32)
    csum = jnp.cumsum(onehot, axis=0)
    rank = jnp.sum(csum * onehot, axis=1) - 1
    counts = csum[-1]
    padded = (counts + MOE_TILE - 1) // MOE_TILE * MOE_TILE
    pad_end = jnp.cumsum(padded)
    pad_start = pad_end - padded
    dest = pad_start[e_flat] + rank
    n_blocks = A // MOE_TILE + N_EXPERTS
    P = n_blocks * MOE_TILE
    tok = jnp.arange(A, dtype=jnp.int32) // 2
    slot_tok = jnp.zeros((P,), jnp.int32).at[dest].set(tok, unique_indices=True, mode='promise_in_bounds')
    block_e = jnp.minimum(
        jnp.searchsorted(pad_end, jnp.arange(n_blocks, dtype=jnp.int32) * MOE_TILE, side='right'),
        N_EXPERTS - 1).astype(jnp.int32)
    n_used = (pad_end[-1] // MOE_TILE).astype(jnp.int32).reshape(1)

    xg = h.reshape(N, D).at[slot_tok].get(mode='promise_in_bounds')
    ys = _moe_ffn(block_e, n_used, xg, w13, w2, layer)
    dest2 = dest.reshape(N, 2)
    ya = ys.at[dest2[:, 0]].get(mode='promise_in_bounds').reshape(B, S, D)
    yb = ys.at[dest2[:, 1]].get(mode='promise_in_bounds').reshape(B, S, D)
    return _combine(x, g, gates, ya, yb, norm_g)


def kernel(x, c, ada_w, ada_b, gla_w_in, gla_w_gate2, gla_b_gate, gla_gn_g, gla_w_out, conv_w_in, conv_b_in, conv_dw, conv_dw_b, conv_ln_g, conv_ln_b, conv_w_out, conv_b_out, fox_w_in, fox_b_f, fox_qn_g, fox_kn_g, fox_w_out, ffn_w13, ffn_w2, moe_router, moe_w13, moe_w2, norm_f_g):
    depth = ada_w.shape[0]
    B = x.shape[0]
    mod = _ada_mod(c, ada_w, ada_b).reshape(depth, 6, B, 1, D)
    ffn_w13, ffn_w2, moe_w13, moe_w2 = [_cast_bf16(w) for w in (ffn_w13, ffn_w2, moe_w13, moe_w2)]
    out = None
    for i in range(depth):
        sh1, sc1, g1, sh2, sc2, g2 = [mod[i, j] for j in range(6)]
        m, j = i % 3, i // 3
        if m == 0:
            x = _gla_layer(x, sc1, sh1, g1, gla_w_in[j], gla_w_gate2[j], gla_b_gate[j],
                           gla_gn_g[j], gla_w_out[j])
        elif m == 1:
            x = _conv_layer(x, sc1, sh1, g1, conv_w_in[j], conv_b_in[j], conv_dw[j], conv_dw_b[j],
                            conv_ln_g[j], conv_ln_b[j], conv_w_out[j], conv_b_out[j])
        else:
            x = _fox_layer(x, sc1, sh1, g1, fox_w_in[j], fox_b_f[j], fox_qn_g[j], fox_kn_g[j],
                           fox_w_out[j])
        last = i == depth - 1
        if i % 2 == 0:
            x = _dense_ffn(x, sc2, sh2, g2, ffn_w13, ffn_w2, i // 2)
            if last:
                out = _final_norm(x, norm_f_g)
        else:
            x = _moe_layer(x, sc2, sh2, g2, moe_router[i // 2], moe_w13, moe_w2, i // 2,
                           norm_f_g if last else None)
            if last:
                out = x
    return out
```

```python
import math

import jax
import jax.numpy as jnp
from jax import lax
from jax.experimental import pallas as pl
from jax.experimental.pallas import tpu as pltpu

F32 = jnp.float32
BF16 = jnp.bfloat16
HIGHEST = lax.Precision.HIGHEST

D = 1024
EPS = 1e-6
CHUNK = 64
GLA_HEADS = 4
GLA_DK = 512
GLA_DV = 1024
GLA_DKH = 128
GLA_DVH = 256
GLA_RANK = 16
GLA_TAU = 16.0
CONV_WIDTH = 31
CONV_HALO = 32
FOX_HEADS = 16
FOX_DH = 64
N_EXPERTS = 8
D_FF = 2816
D_FF_EXPERT = 3584
LANES = 128
SUBLANES = 8
NEG_BIG = -1e30
LOG2E = 1.4426950408889634
VMEM_LIMIT = 56 * 1024 * 1024

ROW_TILE = 512
MOE_TILE = 512
FF_SUB = 256
CAST_BLOCK_ELEMS = 1 << 20


def _cparams(sem):
    return pltpu.CompilerParams(dimension_semantics=sem, vmem_limit_bytes=VMEM_LIMIT)


def _dot(a, b):
    return jnp.dot(a, b, preferred_element_type=F32)


def _dot_nt(a, b):
    return lax.dot_general(a, b, (((1,), (1,)), ((), ())), preferred_element_type=F32)


def _dot_tn(a, b):
    return lax.dot_general(a, b, (((0,), (0,)), ((), ())), preferred_element_type=F32)


def _sigmoid(x):
    return 1.0 / (1.0 + jnp.exp(-x))


def _log_sigmoid(z):
    return jnp.minimum(z, 0.0) - jnp.log(1.0 + jnp.exp(-jnp.abs(z)))


def _norm_mod(x, sc, sh):
    ms = jnp.mean(x * x, axis=-1, keepdims=True)
    return x * lax.rsqrt(ms + EPS) * (1.0 + sc) + sh


def _split3(x):
    hi = x.astype(BF16).astype(F32)
    r1 = x - hi
    mid = r1.astype(BF16).astype(F32)
    lo = (r1 - mid).astype(BF16).astype(F32)
    return hi, mid, lo


def _tri_cumsum(tri, x):
    hi, mid, lo = _split3(x)
    return _dot(tri, hi.astype(BF16)) + _dot(tri, mid.astype(BF16)) + _dot(tri, lo.astype(BF16))


def _lower_tri(n):
    r = lax.broadcasted_iota(jnp.int32, (n, n), 0)
    c = lax.broadcasted_iota(jnp.int32, (n, n), 1)
    return jnp.where(r >= c, 1.0, 0.0).astype(BF16)


def _cast_kernel(x_ref, o_ref):
    o_ref[...] = x_ref[...].astype(BF16)


def _cast_bf16(w):
    cols = w.shape[-1]
    rows = math.prod(w.shape[:-1])
    block = math.gcd(rows, 1 << (CAST_BLOCK_ELEMS // cols).bit_length() - 1)
    spec = pl.BlockSpec((block, cols), lambda i: (i, 0))
    out = pl.pallas_call(
        _cast_kernel,
        grid=(rows // block,),
        in_specs=[spec],
        out_specs=spec,
        out_shape=jax.ShapeDtypeStruct((rows, cols), BF16),
        compiler_params=_cparams(("parallel",)),
        name="cast_bf16",
    )(w.reshape(rows, cols))
    return out.reshape(w.shape)


def _ada_kernel(c_ref, w_ref, b_ref, o_ref):
    c = c_ref[...]
    cond = c * _sigmoid(c)
    o_ref[0, 0] = jnp.dot(cond, w_ref[0], precision=HIGHEST,
                          preferred_element_type=F32) + b_ref[0, 0]


def _ada_mod(c, ada_w, ada_b):
    depth = ada_w.shape[0]
    B = c.shape[0]
    return pl.pallas_call(
        _ada_kernel,
        grid=(depth, 6),
        in_specs=[
            pl.BlockSpec((B, D), lambda l, j: (0, 0)),
            pl.BlockSpec((1, D, D), lambda l, j: (l, 0, j)),
            pl.BlockSpec((1, 1, 1, D), lambda l, j: (l, j, 0, 0)),
        ],
        out_specs=pl.BlockSpec((1, 1, B, D), lambda l, j: (l, j, 0, 0)),
        out_shape=jax.ShapeDtypeStruct((depth, 6, B, D), F32),
        compiler_params=_cparams(("arbitrary", "arbitrary")),
        name="ada_mod",
    )(c, ada_w, ada_b.reshape(depth, 6, 1, D))


def _row_spec(width, col=0):
    return pl.BlockSpec((1, ROW_TILE, width), lambda b, i: (b, i, col))


def _vec_spec(width=D):
    return pl.BlockSpec((1, 1, width), lambda b, i: (b, 0, 0))


def _full_spec(shape):
    return pl.BlockSpec(shape, lambda b, i: (0,) * len(shape))


def _gla_proj_kernel(x_ref, sc_ref, sh_ref, w_ref, wg_ref, w2_ref, bg_ref, o_ref, la_ref):
    h = _norm_mod(x_ref[0], sc_ref[0], sh_ref[0]).astype(BF16)
    n_out = o_ref.shape[2]
    for c in range(0, n_out, 512):
        o_ref[0, :, c:c + 512] = _dot(h, w_ref[:, c:c + 512]).astype(BF16)
    glr = _dot(h, wg_ref[...]).astype(BF16)
    z = _dot(glr, w2_ref[...]) + bg_ref[...]
    la_ref[0] = _log_sigmoid(z) * (1.0 / GLA_TAU)


def _gla_proj(x, sc, sh, w_main, w_glr, w_g2, b_gate):
    B, S, _ = x.shape
    n_out = w_main.shape[1]
    return pl.pallas_call(
        _gla_proj_kernel,
        grid=(B, S // ROW_TILE),
        in_specs=[_row_spec(D), _vec_spec(), _vec_spec(),
                  _full_spec((D, n_out)), _full_spec((D, LANES)),
                  _full_spec((LANES, GLA_DK)), _full_spec((1, GLA_DK))],
        out_specs=[_row_spec(n_out), _row_spec(GLA_DK)],
        out_shape=[jax.ShapeDtypeStruct((B, S, n_out), BF16),
                   jax.ShapeDtypeStruct((B, S, GLA_DK), F32)],
        compiler_params=_cparams(("parallel", "parallel")),
        name="gla_proj",
    )(x, sc, sh, w_main, w_glr, w_g2, b_gate)


def _conv_proj_kernel(x_ref, sc_ref, sh_ref, w_ref, b_ref, o_ref):
    h = _norm_mod(x_ref[0], sc_ref[0], sh_ref[0]).astype(BF16)
    for c in range(0, D, 512):
        a = _dot(h, w_ref[:, c:c + 512]) + b_ref[:, c:c + 512]
        g = _dot(h, w_ref[:, D + c:D + c + 512]) + b_ref[:, D + c:D + c + 512]
        o_ref[0, :, c:c + 512] = (a * _sigmoid(g)).astype(BF16)


def _conv_proj(x, sc, sh, w_in, b_in):
    B, S, _ = x.shape
    return pl.pallas_call(
        _conv_proj_kernel,
        grid=(B, S // ROW_TILE),
        in_specs=[_row_spec(D), _vec_spec(), _vec_spec(),
                  _full_spec((D, 2 * D)), _full_spec((1, 2 * D))],
        out_specs=_row_spec(D),
        out_shape=jax.ShapeDtypeStruct((B, S, D), BF16),
        compiler_params=_cparams(("parallel", "parallel")),
        name="conv_proj",
    )(x, sc, sh, w_in, b_in)


def _fox_proj_kernel(x_ref, sc_ref, sh_ref, w_ref, wf_ref, bf_ref, o_ref, lf_ref):
    h = _norm_mod(x_ref[0], sc_ref[0], sh_ref[0]).astype(BF16)
    n_out = o_ref.shape[2]
    for c in range(0, n_out, 512):
        o_ref[0, :, c:c + 512] = _dot(h, w_ref[:, c:c + 512]).astype(BF16)
    lf_ref[0] = _log_sigmoid(_dot(h, wf_ref[...]) + bf_ref[...])


def _fox_proj(x, sc, sh, w_main, w_fl, b_f):
    B, S, _ = x.shape
    n_out = w_main.shape[1]
    return pl.pallas_call(
        _fox_proj_kernel,
        grid=(B, S // ROW_TILE),
        in_specs=[_row_spec(D), _vec_spec(), _vec_spec(),
                  _full_spec((D, n_out)), _full_spec((D, LANES)), _full_spec((1, LANES))],
        out_specs=[_row_spec(n_out), _row_spec(LANES)],
        out_shape=[jax.ShapeDtypeStruct((B, S, n_out), BF16),
                   jax.ShapeDtypeStruct((B, S, LANES), F32)],
        compiler_params=_cparams(("parallel", "parallel")),
        name="fox_proj",
    )(x, sc, sh, w_main, w_fl, b_f)


def _out_proj_kernel(y_ref, w_ref, b_ref, x_ref, g_ref, o_ref):
    for c in range(0, D, 512):
        y = _dot(y_ref[0], w_ref[:, c:c + 512]) + b_ref[:, c:c + 512]
        o_ref[0, :, c:c + 512] = x_ref[0, :, c:c + 512] + g_ref[0][:, c:c + 512] * y


def _out_proj(y, w, b, x, g):
    B, S, _ = x.shape
    return pl.pallas_call(
        _out_proj_kernel,
        grid=(B, S // ROW_TILE),
        in_specs=[_row_spec(D), _full_spec((D, D)), _full_spec((1, D)), _row_spec(D), _vec_spec()],
        out_specs=_row_spec(D),
        out_shape=jax.ShapeDtypeStruct((B, S, D), F32),
        compiler_params=_cparams(("parallel", "parallel")),
        name="out_proj",
    )(y, w, b, x, g)


def _gla_scan_kernel(q_ref, k_ref, v_ref, r_ref, la_ref, gn_ref, o_ref, st_ref):
    @pl.when(pl.program_id(1) == 0)
    def _():
        st_ref[...] = jnp.zeros_like(st_ref)

    tri = _lower_tri(CHUNK)
    scale = GLA_DKH ** -0.5
    states = [st_ref[h] for h in range(GLA_HEADS)]
    for c in range(ROW_TILE // CHUNK):
        rows = slice(c * CHUNK, (c + 1) * CHUNK)
        cum = _tri_cumsum(tri, la_ref[0, rows, :])
        tot = cum[CHUNK - 1:CHUNK, :]
        kd = (k_ref[0, rows, :].astype(F32) * jnp.exp(tot - cum)).astype(BF16)
        decay = jnp.exp(tot)
        for h in range(GLA_HEADS):
            ks = slice(h * GLA_DKH, (h + 1) * GLA_DKH)
            vs = slice(h * GLA_DVH, (h + 1) * GLA_DVH)
            st = states[h] * decay[:, ks] + _dot_tn(v_ref[0, rows, vs], kd[:, ks])
            states[h] = st
            o = _dot_nt(q_ref[0, rows, ks], st.astype(BF16)) * scale
            ms = jnp.mean(o * o, axis=-1, keepdims=True)
            on = o * lax.rsqrt(ms + EPS) * gn_ref[:, vs]
            r = r_ref[0, rows, vs].astype(F32)
            o_ref[0, rows, vs] = (on * (r * _sigmoid(r))).astype(BF16)
    for h in range(GLA_HEADS):
        st_ref[h] = states[h]


def _gla_scan(qkvr, log_a, gn_g):
    B, S, _ = qkvr.shape
    return pl.pallas_call(
        _gla_scan_kernel,
        grid=(B, S // ROW_TILE),
        in_specs=[_row_spec(GLA_DK, 0), _row_spec(GLA_DK, 1), _row_spec(GLA_DV, 1), _row_spec(GLA_DV, 2),
                  _row_spec(GLA_DK, 0), _full_spec((1, GLA_DV))],
        out_specs=_row_spec(GLA_DV),
        out_shape=jax.ShapeDtypeStruct((B, S, GLA_DV), BF16),
        scratch_shapes=[pltpu.VMEM((GLA_HEADS, GLA_DVH, GLA_DKH), F32)],
        compiler_params=_cparams(("parallel", "arbitrary")),
        name="gla_scan",
    )(qkvr, qkvr, qkvr, qkvr, log_a, gn_g.reshape(1, GLA_DV))


CONV_ROWS = 64
CONV_COLS = 256


def _conv_post_kernel(uc_ref, up_ref, dw_ref, dwb_ref, lng_ref, lnb_ref, w_ref, bo_ref,
                      x_ref, g_ref, o_ref, ext_ref, sh_ref, cv_ref):
    first = pl.program_id(1) == 0
    halo = up_ref[0].astype(F32)
    ext_ref[0:CONV_HALO, :] = jnp.where(first, 0.0, halo)
    ext_ref[CONV_HALO:, :] = uc_ref[0].astype(F32)
    lead = CONV_HALO - (CONV_WIDTH - 1)
    for j in range(1, SUBLANES):
        sh_ref[j - 1] = ext_ref[pl.ds(j, sh_ref.shape[1]), :]

    for c in range(0, D, CONV_COLS):
        def row_group(r, carry, c=c):
            r0 = pl.multiple_of(r * CONV_ROWS, CONV_ROWS)
            acc = jnp.zeros((CONV_ROWS, CONV_COLS), F32) + dwb_ref[:, c:c + CONV_COLS]
            for w in range(CONV_WIDTH):
                j = (lead + w) % SUBLANES
                rows = pl.ds(pl.multiple_of(r0 + (lead + w - j), SUBLANES), CONV_ROWS)
                src = ext_ref if j == 0 else sh_ref.at[j - 1]
                acc = acc + dw_ref[w:w + 1, c:c + CONV_COLS] * src[rows, c:c + CONV_COLS]
            cv_ref[pl.ds(r0, CONV_ROWS), c:c + CONV_COLS] = acc
            return carry
        lax.fori_loop(0, ROW_TILE // CONV_ROWS, row_group, 0)

    u = cv_ref[...]
    mu = jnp.mean(u, axis=-1, keepdims=True)
    var = jnp.mean(jnp.square(u - mu), axis=-1, keepdims=True)
    un = (u - mu) * lax.rsqrt(var + EPS) * lng_ref[...] + lnb_ref[...]
    y = (un * _sigmoid(un)).astype(BF16)
    for c in range(0, D, 512):
        z = _dot(y, w_ref[:, c:c + 512]) + bo_ref[:, c:c + 512]
        o_ref[0, :, c:c + 512] = x_ref[0, :, c:c + 512] + g_ref[0][:, c:c + 512] * z


def _conv_post(u, dw, dw_b, ln_g, ln_b, w_out, b_out, x, g):
    B, S, _ = x.shape
    halo_blocks = ROW_TILE // CONV_HALO
    return pl.pallas_call(
        _conv_post_kernel,
        grid=(B, S // ROW_TILE),
        in_specs=[
            _row_spec(D),
            pl.BlockSpec((1, CONV_HALO, D), lambda b, i: (b, jnp.maximum(i * halo_blocks - 1, 0), 0)),
            _full_spec((CONV_WIDTH, D)), _full_spec((1, D)), _full_spec((1, D)), _full_spec((1, D)),
            _full_spec((D, D)), _full_spec((1, D)), _row_spec(D), _vec_spec(),
        ],
        out_specs=_row_spec(D),
        out_shape=jax.ShapeDtypeStruct((B, S, D), F32),
        scratch_shapes=[pltpu.VMEM((ROW_TILE + CONV_HALO, D), F32),
                        pltpu.VMEM((SUBLANES - 1, ROW_TILE + CONV_HALO - SUBLANES, D), F32),
                        pltpu.VMEM((ROW_TILE, D), F32)],
        compiler_params=_cparams(("parallel", "parallel")),
        name="conv_post",
    )(u, u, dw, dw_b, ln_g, ln_b, w_out, b_out, x, g)


CUM_TILE = 256
PREP_TILE = 2048
FOX_PART_COPIES = 3


def _fox_cum_kernel(lf_ref, o_ref, carry_ref):
    @pl.when(pl.program_id(1) == 0)
    def _():
        carry_ref[...] = jnp.zeros_like(carry_ref)

    cum = _tri_cumsum(_lower_tri(CUM_TILE), lf_ref[0]) + carry_ref[...]
    o_ref[0] = cum
    carry_ref[...] = cum[CUM_TILE - 1:CUM_TILE, :]


def _fox_cum(lf):
    B, S, _ = lf.shape
    spec = pl.BlockSpec((1, CUM_TILE, LANES), lambda b, i: (b, i, 0))
    return pl.pallas_call(
        _fox_cum_kernel,
        grid=(B, S // CUM_TILE),
        in_specs=[spec],
        out_specs=spec,
        out_shape=jax.ShapeDtypeStruct((B, S, LANES), F32),
        scratch_shapes=[pltpu.VMEM((1, LANES), F32)],
        compiler_params=_cparams(("parallel", "arbitrary")),
        name="fox_cum",
    )(lf)


def _fox_prep_kernel(q_ref, k_ref, cum_ref, qg_ref, kg_ref, qa_ref, ka_ref):
    pair = pl.program_id(1)
    lane = lax.broadcasted_iota(jnp.int32, (1, LANES), 1)
    r2 = lax.broadcasted_iota(jnp.int32, (2 * LANES, LANES), 0)
    c2 = lax.broadcasted_iota(jnp.int32, (2 * LANES, LANES), 1)
    same_head = ((r2 & (LANES - 1)) >> 6) == (c2 >> 6)
    head_sum = jnp.where(same_head, 1.0, 0.0).astype(BF16)

    def head_norm(t_ref, g_ref, scale):
        t = t_ref[0].astype(F32)
        sq = t * t
        hi = sq.astype(BF16)
        mid = (sq - hi.astype(F32)).astype(BF16)
        ss = _dot(jnp.concatenate([hi, mid], axis=1), head_sum)
        return (t * lax.rsqrt(ss * (1.0 / FOX_DH) + EPS) * (g_ref[...] * scale)).astype(BF16)

    qb = head_norm(q_ref, qg_ref, LOG2E * FOX_DH ** -0.5)
    kb = head_norm(k_ref, kg_ref, 1.0)
    c_hi, c_mid, c_lo = _split3(cum_ref[0] * LOG2E)
    parts = jnp.where(lane < FOX_HEADS, c_hi,
            jnp.where(lane < 2 * FOX_HEADS, c_mid,
            jnp.where(lane < 3 * FOX_HEADS, c_lo, 0.0))).astype(BF16)
    lhs_q = jnp.concatenate([qb, parts], axis=1)
    lhs_k = jnp.concatenate([kb, parts], axis=1)
    d = FOX_DH
    for hh in range(2):
        h = 2 * pair + hh
        part_row = LANES + h
        src_q = jnp.where(lane < d, lane + hh * d,
                jnp.where(lane < d + 3, part_row + FOX_HEADS * (lane - d), -1))
        src_k = jnp.where(lane < d, lane + hh * d,
                jnp.where((lane >= d + 3) & (lane < d + 6), part_row + FOX_HEADS * (lane - d - 3), -1))
        wq = jnp.where(r2 == src_q, 1.0, 0.0).astype(BF16)
        wk = jnp.where(r2 == src_k, jnp.where(lane < d, 1.0, -1.0), 0.0).astype(BF16)
        ones_q = jnp.where((lane >= d + 3) & (lane < d + 6), 1.0, 0.0)
        ones_k = jnp.where((lane >= d) & (lane < d + 3), 1.0, 0.0)
        qa_ref[0, hh] = (_dot(lhs_q, wq) + ones_q).astype(BF16)
        ka_ref[0, hh] = (_dot(lhs_k, wk) + ones_k).astype(BF16)


def _fox_prep(qkvo, cum, qn_g, kn_g):
    B, S, _ = qkvo.shape
    pairs = FOX_HEADS // 2
    g2 = lambda g: jnp.concatenate([g, g]).reshape(1, LANES)
    tile = min(S, PREP_TILE)
    aug = pl.BlockSpec((1, 2, tile, LANES), lambda b, p, i: (b, p, i, 0))
    return pl.pallas_call(
        _fox_prep_kernel,
        grid=(B, pairs, S // tile),
        in_specs=[
            pl.BlockSpec((1, tile, LANES), lambda b, p, i: (b, i, p)),
            pl.BlockSpec((1, tile, LANES), lambda b, p, i: (b, i, pairs + p)),
            pl.BlockSpec((1, tile, LANES), lambda b, p, i: (b, i, 0)),
            pl.BlockSpec((1, LANES), lambda b, p, i: (0, 0)),
            pl.BlockSpec((1, LANES), lambda b, p, i: (0, 0)),
        ],
        out_specs=[aug, aug],
        out_shape=[jax.ShapeDtypeStruct((B, FOX_HEADS, S, LANES), BF16)] * 2,
        compiler_params=_cparams(("parallel", "parallel", "parallel")),
        name="fox_prep",
    )(qkvo, qkvo, cum, g2(qn_g), g2(kn_g))


ATT_TILE = 512


def _fox_flash_kernel(qt_ref, kt_ref, qa_ref, ka_ref, v_ref, og_ref, o_ref,
                      vx_ref, cap_ref, s0_ref, s1_ref, p0_ref, p1_ref, al0_ref, al1_ref, m_ref, acc_ref):
    T = ATT_TILE
    n_items = qt_ref.shape[0]
    lane = lax.broadcasted_iota(jnp.int32, (1, LANES), 1)
    s_refs, p_refs, al_refs = (s0_ref, s1_ref), (p0_ref, p1_ref), (al0_ref, al1_ref)

    vx_ref[:, :LANES] = v_ref[0]
    ones_col = jnp.where(lane == 0, 1.0, 0.0).astype(BF16)
    vx_ref[:, LANES:] = jnp.broadcast_to(ones_col, (vx_ref.shape[0], LANES))
    causal = (lax.broadcasted_iota(jnp.int32, (T, T), 1) <= lax.broadcasted_iota(jnp.int32, (T, T), 0))
    cap_ref[...] = jnp.where(causal, -NEG_BIG, NEG_BIG)
    for slot in range(2):
        s_refs[slot][...] = jnp.zeros_like(s_refs[slot])
        p_refs[slot][...] = jnp.zeros_like(p_refs[slot])
        al_refs[slot][...] = jnp.ones_like(al_refs[slot])
    m_ref[...] = jnp.full(m_ref.shape, NEG_BIG, F32)
    acc_ref[...] = jnp.zeros_like(acc_ref)

    def block(i):
        return pl.ds(pl.multiple_of(i * T, T), T)

    def tick(t, slot):
        it = jnp.minimum(t, n_items - 1)
        q1, k1 = qt_ref[it], kt_ref[it]
        for hh in range(2):
            s_refs[slot][hh] = _dot_nt(qa_ref[0, hh, block(q1), :], ka_ref[0, hh, block(k1), :])
        it = jnp.clip(t - 1, 0, n_items - 1)
        live = (t >= 1) & (t <= n_items)
        fresh = kt_ref[it] == 0
        for hh in range(2):
            s = s_refs[1 - slot][hh]
            m_old = jnp.where(fresh, NEG_BIG, m_ref[hh])
            m_new = jnp.maximum(m_old, jnp.max(s, axis=-1, keepdims=True))
            m_ref[hh] = jnp.where(live, m_new, m_ref[hh])
            al_refs[1 - slot][hh] = jnp.where(live, jnp.exp2(m_old - m_new), 1.0)
            m_use = jnp.where(live, m_new, -NEG_BIG)
            p_refs[1 - slot][hh] = jnp.exp2(s - pltpu.repeat(m_use, T // LANES, axis=1)).astype(BF16)
        it = jnp.clip(t - 2, 0, n_items - 1)
        q3, k3 = qt_ref[it], kt_ref[it]
        for hh in range(2):
            alpha = pltpu.repeat(al_refs[slot][hh], 2, axis=1)
            acc_ref[hh] = alpha * acc_ref[hh] + _dot(p_refs[slot][hh], vx_ref[block(k3), :])

        @pl.when(q1 == k1)
        def _():
            for hh in range(2):
                s_refs[slot][hh] = jnp.minimum(s_refs[slot][hh], cap_ref[...])

        @pl.when((t >= 2) & (t <= n_items + 1) & (q3 == k3))
        def _():
            heads = [acc_ref[hh, :, :LANES] / acc_ref[hh, :, LANES:LANES + 1] for hh in range(2)]
            o = jnp.where(lane < FOX_DH, heads[0], heads[1])
            gate = _sigmoid(og_ref[0, block(q3), :].astype(F32))
            o_ref[0, block(q3), :] = (o * gate).astype(BF16)

    def two_ticks(i, carry):
        tick(2 * i, 0)
        tick(2 * i + 1, 1)
        return carry

    lax.fori_loop(0, (n_items + 3) // 2, two_ticks, 0)


def _fox_flash(qa, ka, qkvo):
    B, _, S, _ = qa.shape
    T = ATT_TILE
    pairs = FOX_HEADS // 2
    v_off = 2 * pairs
    og_off = 3 * pairs
    items = [(qi, kb) for qi in range(S // T) for kb in range(qi + 1)]
    q_tab = jnp.asarray([qi for qi, _ in items], jnp.int32)
    k_tab = jnp.asarray([kb for _, kb in items], jnp.int32)
    grid_spec = pltpu.PrefetchScalarGridSpec(
        num_scalar_prefetch=2,
        grid=(B, pairs),
        in_specs=[
            pl.BlockSpec((1, 2, S, LANES), lambda b, p, qt, kt: (b, p, 0, 0)),
            pl.BlockSpec((1, 2, S, LANES), lambda b, p, qt, kt: (b, p, 0, 0)),
            pl.BlockSpec((1, S, LANES), lambda b, p, qt, kt: (b, 0, v_off + p)),
            pl.BlockSpec((1, S, LANES), lambda b, p, qt, kt: (b, 0, og_off + p)),
        ],
        out_specs=pl.BlockSpec((1, S, LANES), lambda b, p, qt, kt: (b, 0, p)),
        scratch_shapes=[
            pltpu.VMEM((S, 2 * LANES), BF16),
            pltpu.VMEM((T, T), F32),
            pltpu.VMEM((2, T, T), F32), pltpu.VMEM((2, T, T), F32),
            pltpu.VMEM((2, T, T), BF16), pltpu.VMEM((2, T, T), BF16),
            pltpu.VMEM((2, T, LANES), F32), pltpu.VMEM((2, T, LANES), F32),
            pltpu.VMEM((2, T, LANES), F32),
            pltpu.VMEM((2, T, 2 * LANES), F32),
        ],
    )
    return pl.pallas_call(
        _fox_flash_kernel,
        grid_spec=grid_spec,
        out_shape=jax.ShapeDtypeStruct((B, S, D), BF16),
        compiler_params=_cparams(("parallel", "parallel")),
        name="fox_flash",
    )(q_tab, k_tab, qa, ka, qkvo, qkvo)


def _swiglu_acc(h, w1_ref, w3_ref, w2_ref, acc_ref):
    tf = w2_ref.shape[0]
    for c in range(0, tf, FF_SUB):
        e = min(c + FF_SUB, tf)
        a = _dot(h, w1_ref[:, c:e])
        b = _dot(h, w3_ref[:, c:e])
        act = (a * _sigmoid(a) * b).astype(BF16)
        acc_ref[...] += _dot(act, w2_ref[c:e, :])


def _dense_ffn_kernel(x_ref, sc_ref, sh_ref, g_ref, w1_ref, w3_ref, w2_ref, ea_ref, eb_ref,
                      o_ref, ea_out_ref, eb_out_ref, acc_ref):
    h = _norm_mod(x_ref[0], sc_ref[0], sh_ref[0]).astype(BF16)
    acc_ref[...] = jnp.zeros_like(acc_ref)
    _swiglu_acc(h, w1_ref.at[0], w3_ref.at[0], w2_ref.at[0], acc_ref)
    o_ref[0] = x_ref[0] + g_ref[0] * acc_ref[...]
    ea_out_ref[...] = ea_ref[0].astype(BF16)
    eb_out_ref[...] = eb_ref[0].astype(BF16)


def _dense_ffn(x, sc, sh, g, w13, w2, layer, moe_w13, moe_w2, moe_layer):
    B, S, _ = x.shape
    steps = B * (S // ROW_TILE)
    n_l = moe_w13.shape[0]
    ea = moe_w13.reshape(n_l, N_EXPERTS * D, 2 * D_FF_EXPERT)
    eb = moe_w2.reshape(n_l, N_EXPERTS * D_FF_EXPERT, D)
    ra, rb = ea.shape[1] // steps, eb.shape[1] // steps
    assert ra * steps == ea.shape[1] and rb * steps == eb.shape[1] and ra % 16 == 0 and rb % 16 == 0
    step = lambda b, i: b * (S // ROW_TILE) + i
    once = pl.Buffered(1)
    return pl.pallas_call(
        _dense_ffn_kernel,
        grid=(B, S // ROW_TILE),
        in_specs=[_row_spec(D), _vec_spec(), _vec_spec(), _vec_spec(),
                  pl.BlockSpec((1, D, D_FF), lambda b, i: (layer, 0, 0), pipeline_mode=once),
                  pl.BlockSpec((1, D, D_FF), lambda b, i: (layer, 0, 1), pipeline_mode=once),
                  pl.BlockSpec((1, D_FF, D), lambda b, i: (layer, 0, 0), pipeline_mode=once),
                  pl.BlockSpec((1, ra, ea.shape[2]), lambda b, i: (moe_layer, step(b, i), 0)),
                  pl.BlockSpec((1, rb, D), lambda b, i: (moe_layer, step(b, i), 0))],
        out_specs=[_row_spec(D),
                   pl.BlockSpec((ra, ea.shape[2]), lambda b, i: (step(b, i), 0)),
                   pl.BlockSpec((rb, D), lambda b, i: (step(b, i), 0))],
        out_shape=[jax.ShapeDtypeStruct((B, S, D), F32),
                   jax.ShapeDtypeStruct(ea.shape[1:], BF16),
                   jax.ShapeDtypeStruct(eb.shape[1:], BF16)],
        scratch_shapes=[pltpu.VMEM((ROW_TILE, D), F32)],
        compiler_params=_cparams(("arbitrary", "arbitrary")),
        name="dense_ffn",
    )(x, sc, sh, g, w13, w13, w2, ea, eb)


def _moe_ffn_kernel(be_ref, nb_ref, x_ref, w1_ref, w3_ref, w2_ref, o_ref, acc_ref):
    i = pl.program_id(0)
    f = pl.program_id(1)

    @pl.when(f == 0)
    def _():
        acc_ref[...] = jnp.zeros_like(acc_ref)

    @pl.when(i < nb_ref[0])
    def _():
        _swiglu_acc(x_ref[...], w1_ref.at[0], w3_ref.at[0], w2_ref.at[0], acc_ref)

    @pl.when(f == pl.num_programs(1) - 1)
    def _():
        o_ref[...] = acc_ref[...].astype(BF16)


MOE_FF_TILE = 1792


def _moe_ffn(block_e, n_used, xg, w13, w2):
    P = xg.shape[0]
    tf = MOE_FF_TILE
    nf = D_FF_EXPERT // tf
    last = nf - 1

    def held(i, nb):
        return jnp.minimum(i, nb[0] - 1)

    def fcol(i, f, nb):
        return jnp.where(i < nb[0], f, last)

    grid_spec = pltpu.PrefetchScalarGridSpec(
        num_scalar_prefetch=2,
        grid=(P // MOE_TILE, nf),
        in_specs=[
            pl.BlockSpec((MOE_TILE, D), lambda i, f, be, nb: (held(i, nb), 0)),
            pl.BlockSpec((1, D, tf), lambda i, f, be, nb: (be[i], 0, fcol(i, f, nb))),
            pl.BlockSpec((1, D, tf), lambda i, f, be, nb: (be[i], 0, nf + fcol(i, f, nb))),
            pl.BlockSpec((1, tf, D), lambda i, f, be, nb: (be[i], fcol(i, f, nb), 0)),
        ],
        out_specs=pl.BlockSpec((MOE_TILE, D), lambda i, f, be, nb: (i, 0)),
        scratch_shapes=[pltpu.VMEM((MOE_TILE, D), F32)],
    )
    return pl.pallas_call(
        _moe_ffn_kernel,
        grid_spec=grid_spec,
        out_shape=jax.ShapeDtypeStruct((P, D), BF16),
        compiler_params=_cparams(("arbitrary", "arbitrary")),
        name="moe_ffn",
    )(block_e, n_used, xg, w13, w13, w2)


def _router_kernel(x_ref, sc_ref, sh_ref, wh_ref, wl_ref, h_ref, e_ref, g_ref):
    h = _norm_mod(x_ref[0], sc_ref[0], sh_ref[0])
    h_hi = h.astype(BF16)
    h_lo = (h - h_hi.astype(F32)).astype(BF16)
    h_ref[0] = h_hi
    lane = lax.broadcasted_iota(jnp.int32, (1, LANES), 1)
    lanef = lane.astype(F32)
    lg = _dot(h_hi, wh_ref[...]) + (_dot(h_lo, wh_ref[...]) + _dot(h_hi, wl_ref[...]))
    lg = jnp.where(lane < N_EXPERTS, lg, NEG_BIG)
    m1 = jnp.max(lg, axis=-1, keepdims=True)
    i1 = jnp.min(jnp.where(lg == m1, lanef, float(LANES)), axis=-1, keepdims=True)
    lg2 = jnp.where(lanef == i1, NEG_BIG, lg)
    m2 = jnp.max(lg2, axis=-1, keepdims=True)
    i2 = jnp.min(jnp.where(lg2 == m2, lanef, float(LANES)), axis=-1, keepdims=True)
    e = jnp.exp(m2 - m1)
    g1 = 1.0 / (1.0 + e)
    g2 = e / (1.0 + e)
    e_ref[0] = jnp.where(lane == 0, i1, jnp.where(lane == 1, i2, 0.0)).astype(jnp.int32)
    g_ref[0] = jnp.where(lane == 0, g1, jnp.where(lane == 1, g2, 0.0))


def _router(x, sc, sh, w_router):
    B, S, _ = x.shape
    w_hi = w_router.astype(BF16)
    w_lo = (w_router - w_hi.astype(F32)).astype(BF16)
    return pl.pallas_call(
        _router_kernel,
        grid=(B, S // ROW_TILE),
        in_specs=[_row_spec(D), _vec_spec(), _vec_spec(), _full_spec((D, LANES)), _full_spec((D, LANES))],
        out_specs=[_row_spec(D), _row_spec(LANES), _row_spec(LANES)],
        out_shape=[jax.ShapeDtypeStruct((B, S, D), BF16),
                   jax.ShapeDtypeStruct((B, S, LANES), jnp.int32),
                   jax.ShapeDtypeStruct((B, S, LANES), F32)],
        compiler_params=_cparams(("parallel", "parallel")),
        name="router",
    )(x, sc, sh, w_hi, w_lo)


def _combined(x_ref, g_ref, gate_ref, ya_ref, yb_ref):
    gate = gate_ref[0]
    y = ya_ref[0] * gate[:, 0:1] + yb_ref[0] * gate[:, 1:2]
    return x_ref[0] + g_ref[0] * y


def _combine_kernel(x_ref, g_ref, gate_ref, ya_ref, yb_ref, o_ref):
    o_ref[0] = _combined(x_ref, g_ref, gate_ref, ya_ref, yb_ref)


def _combine_norm_kernel(x_ref, g_ref, gate_ref, ya_ref, yb_ref, gf_ref, o_ref):
    x = _combined(x_ref, g_ref, gate_ref, ya_ref, yb_ref)
    ms = jnp.mean(x * x, axis=-1, keepdims=True)
    o_ref[0] = x * lax.rsqrt(ms + EPS) * gf_ref[...]


def _combine(x, g, gates, ya, yb, norm_g=None):
    B, S, _ = x.shape
    in_specs = [_row_spec(D), _vec_spec(), _row_spec(LANES), _row_spec(D), _row_spec(D)]
    args = [x, g, gates, ya, yb]
    body = _combine_kernel
    if norm_g is not None:
        in_specs.append(_full_spec((1, D)))
        args.append(norm_g.reshape(1, D))
        body = _combine_norm_kernel
    return pl.pallas_call(
        body,
        grid=(B, S // ROW_TILE),
        in_specs=in_specs,
        out_specs=_row_spec(D),
        out_shape=jax.ShapeDtypeStruct((B, S, D), F32),
        compiler_params=_cparams(("parallel", "parallel")),
        name="moe_combine",
    )(*args)


def _pad_cols(w, width):
    return jnp.pad(w, ((0, 0), (0, width - w.shape[1])))


def _gla_layer(x, sc, sh, g, w_in, w_gate2, b_gate, gn_g, w_out):
    n_main = 2 * GLA_DK + 2 * GLA_DV
    w_main = w_in[:, :n_main].astype(BF16)
    w_glr = _pad_cols(w_in[:, n_main:], LANES).astype(BF16)
    w_g2 = jnp.pad(w_gate2, ((0, LANES - GLA_RANK), (0, 0))).astype(BF16)
    qkvr, log_a = _gla_proj(x, sc, sh, w_main, w_glr, w_g2, b_gate.reshape(1, GLA_DK))
    y = _gla_scan(qkvr, log_a, gn_g)
    return _out_proj(y, w_out.astype(BF16), jnp.zeros((1, D), F32), x, g)


def _conv_layer(x, sc, sh, g, w_in, b_in, dw, dw_b, ln_g, ln_b, w_out, b_out):
    u = _conv_proj(x, sc, sh, w_in.astype(BF16), b_in.reshape(1, 2 * D))
    return _conv_post(u, dw, dw_b.reshape(1, D), ln_g.reshape(1, D), ln_b.reshape(1, D),
                      w_out.astype(BF16), b_out.reshape(1, D), x, g)


def _fox_layer(x, sc, sh, g, w_in, b_f, qn_g, kn_g, w_out):
    w_main = jnp.concatenate([w_in[:, :3 * D], w_in[:, 3 * D + FOX_HEADS:]], axis=1).astype(BF16)
    w_fl = _pad_cols(jnp.tile(w_in[:, 3 * D:3 * D + FOX_HEADS], (1, FOX_PART_COPIES)), LANES).astype(BF16)
    b_fl = _pad_cols(jnp.tile(b_f.reshape(1, FOX_HEADS), (1, FOX_PART_COPIES)), LANES)
    qkvo, lf = _fox_proj(x, sc, sh, w_main, w_fl, b_fl)
    cum = _fox_cum(lf)
    qa, ka = _fox_prep(qkvo, cum, qn_g, kn_g)
    y = _fox_flash(qa, ka, qkvo)
    return _out_proj(y, w_out.astype(BF16), jnp.zeros((1, D), F32), x, g)


def _moe_layer(x, sc, sh, g, w_router, w13, w2, norm_g):
    B, S, _ = x.shape
    w13 = w13.reshape(N_EXPERTS, D, 2 * D_FF_EXPERT)
    w2 = w2.reshape(N_EXPERTS, D_FF_EXPERT, D)
    N = B * S
    A = 2 * N
    h, e_pad, gates = _router(x, sc, sh, _pad_cols(w_router, LANES))
    e_flat = e_pad[:, :, :2].reshape(A)

    onehot = (e_flat[:, None] == jnp.arange(N_EXPERTS, dtype=jnp.int32)[None, :]).astype(jnp.int32)
    csum = jnp.cumsum(onehot, axis=0)
    rank = jnp.sum(csum * onehot, axis=1) - 1
    counts = csum[-1]
    padded = (counts + MOE_TILE - 1) // MOE_TILE * MOE_TILE
    pad_end = jnp.cumsum(padded)
    pad_start = pad_end - padded
    dest = pad_start[e_flat] + rank
    n_blocks = A // MOE_TILE + N_EXPERTS
    P = n_blocks * MOE_TILE
    block_e = jnp.minimum(
        jnp.searchsorted(pad_end, jnp.arange(n_blocks, dtype=jnp.int32) * MOE_TILE, side='right'),
        N_EXPERTS - 1).astype(jnp.int32)
    n_used = (pad_end[-1] // MOE_TILE).astype(jnp.int32).reshape(1)
    tok = jnp.arange(A, dtype=jnp.int32) // 2
    _, tok_sorted = lax.sort_key_val(dest, tok)
    gap = pad_start - (jnp.cumsum(counts) - counts)
    slot = jnp.arange(P, dtype=jnp.int32)
    compact = jnp.clip(slot - gap[block_e[slot // MOE_TILE]], 0, A - 1)
    slot_tok = tok_sorted.at[compact].get(mode='promise_in_bounds')

    xg = h.reshape(N, D).at[slot_tok].get(mode='promise_in_bounds')
    ys = _moe_ffn(block_e, n_used, xg, w13, w2)
    dest2 = dest.reshape(N, 2)
    ya = ys.at[dest2[:, 0]].get(mode='promise_in_bounds').reshape(B, S, D)
    yb = ys.at[dest2[:, 1]].get(mode='promise_in_bounds').reshape(B, S, D)
    return _combine(x, g, gates, ya, yb, norm_g)


def kernel(x, c, ada_w, ada_b, gla_w_in, gla_w_gate2, gla_b_gate, gla_gn_g, gla_w_out, conv_w_in, conv_b_in, conv_dw, conv_dw_b, conv_ln_g, conv_ln_b, conv_w_out, conv_b_out, fox_w_in, fox_b_f, fox_qn_g, fox_kn_g, fox_w_out, ffn_w13, ffn_w2, moe_router, moe_w13, moe_w2, norm_f_g):
    depth = ada_w.shape[0]
    B = x.shape[0]
    mod = _ada_mod(c, ada_w, ada_b).reshape(depth, 6, B, 1, D)
    assert depth % 2 == 0
    ffn_w13, ffn_w2 = _cast_bf16(ffn_w13), _cast_bf16(ffn_w2)
    expert_w = None
    for i in range(depth):
        sh1, sc1, g1, sh2, sc2, g2 = [mod[i, j] for j in range(6)]
        m, j = i % 3, i // 3
        if m == 0:
            x = _gla_layer(x, sc1, sh1, g1, gla_w_in[j], gla_w_gate2[j], gla_b_gate[j],
                           gla_gn_g[j], gla_w_out[j])
        elif m == 1:
            x = _conv_layer(x, sc1, sh1, g1, conv_w_in[j], conv_b_in[j], conv_dw[j], conv_dw_b[j],
                            conv_ln_g[j], conv_ln_b[j], conv_w_out[j], conv_b_out[j])
        else:
            x = _fox_layer(x, sc1, sh1, g1, fox_w_in[j], fox_b_f[j], fox_qn_g[j], fox_kn_g[j],
                           fox_w_out[j])
        if i % 2 == 0:
            x, *expert_w = _dense_ffn(x, sc2, sh2, g2, ffn_w13, ffn_w2, i // 2, moe_w13, moe_w2, i // 2)
        else:
            x = _moe_layer(x, sc2, sh2, g2, moe_router[i // 2], *expert_w,
                           norm_f_g if i == depth - 1 else None)
    return x
```

```python
import math

import jax
import jax.numpy as jnp
from jax import lax
from jax.experimental import pallas as pl
from jax.experimental.pallas import tpu as pltpu

F32 = jnp.float32
BF16 = jnp.bfloat16
HIGHEST = lax.Precision.HIGHEST

D = 1024
EPS = 1e-6
CHUNK = 64
GLA_HEADS = 4
GLA_DK = 512
GLA_DV = 1024
GLA_DKH = 128
GLA_DVH = 256
GLA_RANK = 16
GLA_TAU = 16.0
CONV_WIDTH = 31
CONV_HALO = 32
FOX_HEADS = 16
FOX_DH = 64
N_EXPERTS = 8
D_FF = 2816
D_FF_EXPERT = 3584
LANES = 128
SUBLANES = 8
NEG_BIG = -1e30
LOG2E = 1.4426950408889634
VMEM_LIMIT = 56 * 1024 * 1024

ROW_TILE = 512
MOE_TILE = 512
FF_SUB = 256
CAST_BLOCK_ELEMS = 1 << 20


def _cparams(sem):
    return pltpu.CompilerParams(dimension_semantics=sem, vmem_limit_bytes=VMEM_LIMIT)


def _dot(a, b):
    return jnp.dot(a, b, preferred_element_type=F32)


def _dot_nt(a, b):
    return lax.dot_general(a, b, (((1,), (1,)), ((), ())), preferred_element_type=F32)


def _dot_tn(a, b):
    return lax.dot_general(a, b, (((0,), (0,)), ((), ())), preferred_element_type=F32)


def _sigmoid(x):
    return 1.0 / (1.0 + jnp.exp(-x))


def _log_sigmoid(z):
    return jnp.minimum(z, 0.0) - jnp.log(1.0 + jnp.exp(-jnp.abs(z)))


def _norm_mod(x, sc, sh):
    ms = jnp.mean(x * x, axis=-1, keepdims=True)
    return x * lax.rsqrt(ms + EPS) * (1.0 + sc) + sh


def _split3(x):
    hi = x.astype(BF16).astype(F32)
    r1 = x - hi
    mid = r1.astype(BF16).astype(F32)
    lo = (r1 - mid).astype(BF16).astype(F32)
    return hi, mid, lo


def _tri_cumsum(tri, x):
    hi, mid, lo = _split3(x)
    return _dot(tri, hi.astype(BF16)) + _dot(tri, mid.astype(BF16)) + _dot(tri, lo.astype(BF16))


def _lower_tri(n):
    r = lax.broadcasted_iota(jnp.int32, (n, n), 0)
    c = lax.broadcasted_iota(jnp.int32, (n, n), 1)
    return jnp.where(r >= c, 1.0, 0.0).astype(BF16)


def _cast_kernel(x_ref, o_ref):
    o_ref[...] = x_ref[...].astype(BF16)


def _cast_bf16(w):
    cols = w.shape[-1]
    rows = math.prod(w.shape[:-1])
    block = math.gcd(rows, 1 << (CAST_BLOCK_ELEMS // cols).bit_length() - 1)
    spec = pl.BlockSpec((block, cols), lambda i: (i, 0))
    out = pl.pallas_call(
        _cast_kernel,
        grid=(rows // block,),
        in_specs=[spec],
        out_specs=spec,
        out_shape=jax.ShapeDtypeStruct((rows, cols), BF16),
        compiler_params=_cparams(("parallel",)),
        name="cast_bf16",
    )(w.reshape(rows, cols))
    return out.reshape(w.shape)


def _ada_kernel(c_ref, w_ref, b_ref, o_ref):
    c = c_ref[...]
    cond = c * _sigmoid(c)
    o_ref[0, 0] = jnp.dot(cond, w_ref[0], precision=HIGHEST,
                          preferred_element_type=F32) + b_ref[0, 0]


def _ada_mod(c, ada_w, ada_b):
    depth = ada_w.shape[0]
    B = c.shape[0]
    return pl.pallas_call(
        _ada_kernel,
        grid=(depth, 6),
        in_specs=[
            pl.BlockSpec((B, D), lambda l, j: (0, 0)),
            pl.BlockSpec((1, D, D), lambda l, j: (l, 0, j)),
            pl.BlockSpec((1, 1, 1, D), lambda l, j: (l, j, 0, 0)),
        ],
        out_specs=pl.BlockSpec((1, 1, B, D), lambda l, j: (l, j, 0, 0)),
        out_shape=jax.ShapeDtypeStruct((depth, 6, B, D), F32),
        compiler_params=_cparams(("arbitrary", "arbitrary")),
        name="ada_mod",
    )(c, ada_w, ada_b.reshape(depth, 6, 1, D))


def _row_spec(width, col=0):
    return pl.BlockSpec((1, ROW_TILE, width), lambda b, i: (b, i, col))


def _vec_spec(width=D):
    return pl.BlockSpec((1, 1, width), lambda b, i: (b, 0, 0))


def _full_spec(shape):
    return pl.BlockSpec(shape, lambda b, i: (0,) * len(shape))


def _gla_proj_kernel(x_ref, sc_ref, sh_ref, w_ref, wg_ref, w2_ref, bg_ref, o_ref, la_ref):
    h = _norm_mod(x_ref[0], sc_ref[0], sh_ref[0]).astype(BF16)
    n_out = o_ref.shape[2]
    for c in range(0, n_out, 512):
        o_ref[0, :, c:c + 512] = _dot(h, w_ref[:, c:c + 512]).astype(BF16)
    glr = _dot(h, wg_ref[...]).astype(BF16)
    z = _dot(glr, w2_ref[...]) + bg_ref[...]
    la_ref[0] = _log_sigmoid(z) * (1.0 / GLA_TAU)


def _gla_proj(x, sc, sh, w_main, w_glr, w_g2, b_gate):
    B, S, _ = x.shape
    n_out = w_main.shape[1]
    return pl.pallas_call(
        _gla_proj_kernel,
        grid=(B, S // ROW_TILE),
        in_specs=[_row_spec(D), _vec_spec(), _vec_spec(),
                  _full_spec((D, n_out)), _full_spec((D, LANES)),
                  _full_spec((LANES, GLA_DK)), _full_spec((1, GLA_DK))],
        out_specs=[_row_spec(n_out), _row_spec(GLA_DK)],
        out_shape=[jax.ShapeDtypeStruct((B, S, n_out), BF16),
                   jax.ShapeDtypeStruct((B, S, GLA_DK), F32)],
        compiler_params=_cparams(("parallel", "parallel")),
        name="gla_proj",
    )(x, sc, sh, w_main, w_glr, w_g2, b_gate)


def _conv_proj_kernel(x_ref, sc_ref, sh_ref, w_ref, b_ref, o_ref):
    h = _norm_mod(x_ref[0], sc_ref[0], sh_ref[0]).astype(BF16)
    for c in range(0, D, 512):
        a = _dot(h, w_ref[:, c:c + 512]) + b_ref[:, c:c + 512]
        g = _dot(h, w_ref[:, D + c:D + c + 512]) + b_ref[:, D + c:D + c + 512]
        o_ref[0, :, c:c + 512] = (a * _sigmoid(g)).astype(BF16)


def _conv_proj(x, sc, sh, w_in, b_in):
    B, S, _ = x.shape
    return pl.pallas_call(
        _conv_proj_kernel,
        grid=(B, S // ROW_TILE),
        in_specs=[_row_spec(D), _vec_spec(), _vec_spec(),
                  _full_spec((D, 2 * D)), _full_spec((1, 2 * D))],
        out_specs=_row_spec(D),
        out_shape=jax.ShapeDtypeStruct((B, S, D), BF16),
        compiler_params=_cparams(("parallel", "parallel")),
        name="conv_proj",
    )(x, sc, sh, w_in, b_in)


def _fox_proj_kernel(x_ref, sc_ref, sh_ref, w_ref, wf_ref, bf_ref, o_ref, lf_ref):
    h = _norm_mod(x_ref[0], sc_ref[0], sh_ref[0]).astype(BF16)
    n_out = o_ref.shape[2]
    for c in range(0, n_out, 512):
        o_ref[0, :, c:c + 512] = _dot(h, w_ref[:, c:c + 512]).astype(BF16)
    lf_ref[0] = _log_sigmoid(_dot(h, wf_ref[...]) + bf_ref[...])


def _fox_proj(x, sc, sh, w_main, w_fl, b_f):
    B, S, _ = x.shape
    n_out = w_main.shape[1]
    return pl.pallas_call(
        _fox_proj_kernel,
        grid=(B, S // ROW_TILE),
        in_specs=[_row_spec(D), _vec_spec(), _vec_spec(),
                  _full_spec((D, n_out)), _full_spec((D, LANES)), _full_spec((1, LANES))],
        out_specs=[_row_spec(n_out), _row_spec(LANES)],
        out_shape=[jax.ShapeDtypeStruct((B, S, n_out), BF16),
                   jax.ShapeDtypeStruct((B, S, LANES), F32)],
        compiler_params=_cparams(("parallel", "parallel")),
        name="fox_proj",
    )(x, sc, sh, w_main, w_fl, b_f)


def _out_proj_kernel(y_ref, w_ref, b_ref, x_ref, g_ref, o_ref):
    for c in range(0, D, 512):
        y = _dot(y_ref[0], w_ref[:, c:c + 512]) + b_ref[:, c:c + 512]
        o_ref[0, :, c:c + 512] = x_ref[0, :, c:c + 512] + g_ref[0][:, c:c + 512] * y


def _out_proj(y, w, b, x, g):
    B, S, _ = x.shape
    return pl.pallas_call(
        _out_proj_kernel,
        grid=(B, S // ROW_TILE),
        in_specs=[_row_spec(D), _full_spec((D, D)), _full_spec((1, D)), _row_spec(D), _vec_spec()],
        out_specs=_row_spec(D),
        out_shape=jax.ShapeDtypeStruct((B, S, D), F32),
        compiler_params=_cparams(("parallel", "parallel")),
        name="out_proj",
    )(y, w, b, x, g)


def _gla_scan_kernel(q_ref, k_ref, v_ref, r_ref, la_ref, gn_ref, o_ref, st_ref):
    @pl.when(pl.program_id(1) == 0)
    def _():
        st_ref[...] = jnp.zeros_like(st_ref)

    tri = _lower_tri(CHUNK)
    scale = GLA_DKH ** -0.5
    states = [st_ref[h] for h in range(GLA_HEADS)]
    for c in range(ROW_TILE // CHUNK):
        rows = slice(c * CHUNK, (c + 1) * CHUNK)
        cum = _tri_cumsum(tri, la_ref[0, rows, :])
        tot = cum[CHUNK - 1:CHUNK, :]
        kd = (k_ref[0, rows, :].astype(F32) * jnp.exp(tot - cum)).astype(BF16)
        decay = jnp.exp(tot)
        for h in range(GLA_HEADS):
            ks = slice(h * GLA_DKH, (h + 1) * GLA_DKH)
            vs = slice(h * GLA_DVH, (h + 1) * GLA_DVH)
            st = states[h] * decay[:, ks] + _dot_tn(v_ref[0, rows, vs], kd[:, ks])
            states[h] = st
            o = _dot_nt(q_ref[0, rows, ks], st.astype(BF16)) * scale
            ms = jnp.mean(o * o, axis=-1, keepdims=True)
            on = o * lax.rsqrt(ms + EPS) * gn_ref[:, vs]
            r = r_ref[0, rows, vs].astype(F32)
            o_ref[0, rows, vs] = (on * (r * _sigmoid(r))).astype(BF16)
    for h in range(GLA_HEADS):
        st_ref[h] = states[h]


def _gla_scan(qkvr, log_a, gn_g):
    B, S, _ = qkvr.shape
    return pl.pallas_call(
        _gla_scan_kernel,
        grid=(B, S // ROW_TILE),
        in_specs=[_row_spec(GLA_DK, 0), _row_spec(GLA_DK, 1), _row_spec(GLA_DV, 1), _row_spec(GLA_DV, 2),
                  _row_spec(GLA_DK, 0), _full_spec((1, GLA_DV))],
        out_specs=_row_spec(GLA_DV),
        out_shape=jax.ShapeDtypeStruct((B, S, GLA_DV), BF16),
        scratch_shapes=[pltpu.VMEM((GLA_HEADS, GLA_DVH, GLA_DKH), F32)],
        compiler_params=_cparams(("parallel", "arbitrary")),
        name="gla_scan",
    )(qkvr, qkvr, qkvr, qkvr, log_a, gn_g.reshape(1, GLA_DV))


CONV_ROWS = 64
CONV_COLS = 256


def _conv_post_kernel(uc_ref, up_ref, dw_ref, dwb_ref, lng_ref, lnb_ref, w_ref, bo_ref,
                      x_ref, g_ref, o_ref, ext_ref, sh_ref, cv_ref):
    first = pl.program_id(1) == 0
    halo = up_ref[0].astype(F32)
    ext_ref[0:CONV_HALO, :] = jnp.where(first, 0.0, halo)
    ext_ref[CONV_HALO:, :] = uc_ref[0].astype(F32)
    lead = CONV_HALO - (CONV_WIDTH - 1)
    for j in range(1, SUBLANES):
        sh_ref[j - 1] = ext_ref[pl.ds(j, sh_ref.shape[1]), :]

    for c in range(0, D, CONV_COLS):
        def row_group(r, carry, c=c):
            r0 = pl.multiple_of(r * CONV_ROWS, CONV_ROWS)
            acc = jnp.zeros((CONV_ROWS, CONV_COLS), F32) + dwb_ref[:, c:c + CONV_COLS]
            for w in range(CONV_WIDTH):
                j = (lead + w) % SUBLANES
                rows = pl.ds(pl.multiple_of(r0 + (lead + w - j), SUBLANES), CONV_ROWS)
                src = ext_ref if j == 0 else sh_ref.at[j - 1]
                acc = acc + dw_ref[w:w + 1, c:c + CONV_COLS] * src[rows, c:c + CONV_COLS]
            cv_ref[pl.ds(r0, CONV_ROWS), c:c + CONV_COLS] = acc
            return carry
        lax.fori_loop(0, ROW_TILE // CONV_ROWS, row_group, 0)

    u = cv_ref[...]
    mu = jnp.mean(u, axis=-1, keepdims=True)
    var = jnp.mean(jnp.square(u - mu), axis=-1, keepdims=True)
    un = (u - mu) * lax.rsqrt(var + EPS) * lng_ref[...] + lnb_ref[...]
    y = (un * _sigmoid(un)).astype(BF16)
    for c in range(0, D, 512):
        z = _dot(y, w_ref[:, c:c + 512]) + bo_ref[:, c:c + 512]
        o_ref[0, :, c:c + 512] = x_ref[0, :, c:c + 512] + g_ref[0][:, c:c + 512] * z


def _conv_post(u, dw, dw_b, ln_g, ln_b, w_out, b_out, x, g):
    B, S, _ = x.shape
    halo_blocks = ROW_TILE // CONV_HALO
    return pl.pallas_call(
        _conv_post_kernel,
        grid=(B, S // ROW_TILE),
        in_specs=[
            _row_spec(D),
            pl.BlockSpec((1, CONV_HALO, D), lambda b, i: (b, jnp.maximum(i * halo_blocks - 1, 0), 0)),
            _full_spec((CONV_WIDTH, D)), _full_spec((1, D)), _full_spec((1, D)), _full_spec((1, D)),
            _full_spec((D, D)), _full_spec((1, D)), _row_spec(D), _vec_spec(),
        ],
        out_specs=_row_spec(D),
        out_shape=jax.ShapeDtypeStruct((B, S, D), F32),
        scratch_shapes=[pltpu.VMEM((ROW_TILE + CONV_HALO, D), F32),
                        pltpu.VMEM((SUBLANES - 1, ROW_TILE + CONV_HALO - SUBLANES, D), F32),
                        pltpu.VMEM((ROW_TILE, D), F32)],
        compiler_params=_cparams(("parallel", "parallel")),
        name="conv_post",
    )(u, u, dw, dw_b, ln_g, ln_b, w_out, b_out, x, g)


CUM_TILE = 256
PREP_TILE = 2048
FOX_PART_COPIES = 3


def _fox_cum_kernel(lf_ref, o_ref, carry_ref):
    @pl.when(pl.program_id(1) == 0)
    def _():
        carry_ref[...] = jnp.zeros_like(carry_ref)

    cum = _tri_cumsum(_lower_tri(CUM_TILE), lf_ref[0]) + carry_ref[...]
    o_ref[0] = cum
    carry_ref[...] = cum[CUM_TILE - 1:CUM_TILE, :]


def _fox_cum(lf):
    B, S, _ = lf.shape
    spec = pl.BlockSpec((1, CUM_TILE, LANES), lambda b, i: (b, i, 0))
    return pl.pallas_call(
        _fox_cum_kernel,
        grid=(B, S // CUM_TILE),
        in_specs=[spec],
        out_specs=spec,
        out_shape=jax.ShapeDtypeStruct((B, S, LANES), F32),
        scratch_shapes=[pltpu.VMEM((1, LANES), F32)],
        compiler_params=_cparams(("parallel", "arbitrary")),
        name="fox_cum",
    )(lf)


def _fox_prep_kernel(q_ref, k_ref, cum_ref, qg_ref, kg_ref, qa_ref, ka_ref):
    pair = pl.program_id(1)
    lane = lax.broadcasted_iota(jnp.int32, (1, LANES), 1)
    r2 = lax.broadcasted_iota(jnp.int32, (2 * LANES, LANES), 0)
    c2 = lax.broadcasted_iota(jnp.int32, (2 * LANES, LANES), 1)
    same_head = ((r2 & (LANES - 1)) >> 6) == (c2 >> 6)
    head_sum = jnp.where(same_head, 1.0, 0.0).astype(BF16)

    def head_norm(t_ref, g_ref, scale):
        t = t_ref[0].astype(F32)
        sq = t * t
        hi = sq.astype(BF16)
        mid = (sq - hi.astype(F32)).astype(BF16)
        ss = _dot(jnp.concatenate([hi, mid], axis=1), head_sum)
        return (t * lax.rsqrt(ss * (1.0 / FOX_DH) + EPS) * (g_ref[...] * scale)).astype(BF16)

    qb = head_norm(q_ref, qg_ref, LOG2E * FOX_DH ** -0.5)
    kb = head_norm(k_ref, kg_ref, 1.0)
    c_hi, c_mid, c_lo = _split3(cum_ref[0] * LOG2E)
    parts = jnp.where(lane < FOX_HEADS, c_hi,
            jnp.where(lane < 2 * FOX_HEADS, c_mid,
            jnp.where(lane < 3 * FOX_HEADS, c_lo, 0.0))).astype(BF16)
    lhs_q = jnp.concatenate([qb, parts], axis=1)
    lhs_k = jnp.concatenate([kb, parts], axis=1)
    d = FOX_DH
    for hh in range(2):
        h = 2 * pair + hh
        part_row = LANES + h
        src_q = jnp.where(lane < d, lane + hh * d,
                jnp.where(lane < d + 3, part_row + FOX_HEADS * (lane - d), -1))
        src_k = jnp.where(lane < d, lane + hh * d,
                jnp.where((lane >= d + 3) & (lane < d + 6), part_row + FOX_HEADS * (lane - d - 3), -1))
        wq = jnp.where(r2 == src_q, 1.0, 0.0).astype(BF16)
        wk = jnp.where(r2 == src_k, jnp.where(lane < d, 1.0, -1.0), 0.0).astype(BF16)
        ones_q = jnp.where((lane >= d + 3) & (lane < d + 6), 1.0, 0.0)
        ones_k = jnp.where((lane >= d) & (lane < d + 3), 1.0, 0.0)
        qa_ref[0, hh] = (_dot(lhs_q, wq) + ones_q).astype(BF16)
        ka_ref[0, hh] = (_dot(lhs_k, wk) + ones_k).astype(BF16)


def _fox_prep(qkvo, cum, qn_g, kn_g):
    B, S, _ = qkvo.shape
    pairs = FOX_HEADS // 2
    g2 = lambda g: jnp.concatenate([g, g]).reshape(1, LANES)
    tile = min(S, PREP_TILE)
    aug = pl.BlockSpec((1, 2, tile, LANES), lambda b, p, i: (b, p, i, 0))
    return pl.pallas_call(
        _fox_prep_kernel,
        grid=(B, pairs, S // tile),
        in_specs=[
            pl.BlockSpec((1, tile, LANES), lambda b, p, i: (b, i, p)),
            pl.BlockSpec((1, tile, LANES), lambda b, p, i: (b, i, pairs + p)),
            pl.BlockSpec((1, tile, LANES), lambda b, p, i: (b, i, 0)),
            pl.BlockSpec((1, LANES), lambda b, p, i: (0, 0)),
            pl.BlockSpec((1, LANES), lambda b, p, i: (0, 0)),
        ],
        out_specs=[aug, aug],
        out_shape=[jax.ShapeDtypeStruct((B, FOX_HEADS, S, LANES), BF16)] * 2,
        compiler_params=_cparams(("parallel", "parallel", "parallel")),
        name="fox_prep",
    )(qkvo, qkvo, cum, g2(qn_g), g2(kn_g))


ATT_TILE = 512


def _fox_flash_kernel(qt_ref, kt_ref, qa_ref, ka_ref, v_ref, og_ref, o_ref,
                      vx_ref, cap_ref, s0_ref, s1_ref, p0_ref, p1_ref, al0_ref, al1_ref, m_ref, acc_ref):
    T = ATT_TILE
    n_items = qt_ref.shape[0]
    lane = lax.broadcasted_iota(jnp.int32, (1, LANES), 1)
    s_refs, p_refs, al_refs = (s0_ref, s1_ref), (p0_ref, p1_ref), (al0_ref, al1_ref)

    vx_ref[:, :LANES] = v_ref[0]
    ones_col = jnp.where(lane == 0, 1.0, 0.0).astype(BF16)
    vx_ref[:, LANES:] = jnp.broadcast_to(ones_col, (vx_ref.shape[0], LANES))
    causal = (lax.broadcasted_iota(jnp.int32, (T, T), 1) <= lax.broadcasted_iota(jnp.int32, (T, T), 0))
    cap_ref[...] = jnp.where(causal, -NEG_BIG, NEG_BIG)
    for slot in range(2):
        s_refs[slot][...] = jnp.zeros_like(s_refs[slot])
        p_refs[slot][...] = jnp.zeros_like(p_refs[slot])
        al_refs[slot][...] = jnp.ones_like(al_refs[slot])
    m_ref[...] = jnp.full(m_ref.shape, NEG_BIG, F32)
    acc_ref[...] = jnp.zeros_like(acc_ref)

    def block(i):
        return pl.ds(pl.multiple_of(i * T, T), T)

    def tick(t, slot):
        it = jnp.minimum(t, n_items - 1)
        q1, k1 = qt_ref[it], kt_ref[it]
        for hh in range(2):
            s_refs[slot][hh] = _dot_nt(qa_ref[0, hh, block(q1), :], ka_ref[0, hh, block(k1), :])
        it = jnp.clip(t - 1, 0, n_items - 1)
        live = (t >= 1) & (t <= n_items)
        fresh = kt_ref[it] == 0
        for hh in range(2):
            s = s_refs[1 - slot][hh]
            m_old = jnp.where(fresh, NEG_BIG, m_ref[hh])
            m_new = jnp.maximum(m_old, jnp.max(s, axis=-1, keepdims=True))
            m_ref[hh] = jnp.where(live, m_new, m_ref[hh])
            al_refs[1 - slot][hh] = jnp.where(live, jnp.exp2(m_old - m_new), 1.0)
            m_use = jnp.where(live, m_new, -NEG_BIG)
            p_refs[1 - slot][hh] = jnp.exp2(s - pltpu.repeat(m_use, T // LANES, axis=1)).astype(BF16)
        it = jnp.clip(t - 2, 0, n_items - 1)
        q3, k3 = qt_ref[it], kt_ref[it]
        for hh in range(2):
            alpha = pltpu.repeat(al_refs[slot][hh], 2, axis=1)
            acc_ref[hh] = alpha * acc_ref[hh] + _dot(p_refs[slot][hh], vx_ref[block(k3), :])

        @pl.when(q1 == k1)
        def _():
            for hh in range(2):
                s_refs[slot][hh] = jnp.minimum(s_refs[slot][hh], cap_ref[...])

        @pl.when((t >= 2) & (t <= n_items + 1) & (q3 == k3))
        def _():
            heads = [acc_ref[hh, :, :LANES] / acc_ref[hh, :, LANES:LANES + 1] for hh in range(2)]
            o = jnp.where(lane < FOX_DH, heads[0], heads[1])
            gate = _sigmoid(og_ref[0, block(q3), :].astype(F32))
            o_ref[0, block(q3), :] = (o * gate).astype(BF16)

    def two_ticks(i, carry):
        tick(2 * i, 0)
        tick(2 * i + 1, 1)
        return carry

    lax.fori_loop(0, (n_items + 3) // 2, two_ticks, 0)


def _fox_flash(qa, ka, qkvo):
    B, _, S, _ = qa.shape
    T = ATT_TILE
    pairs = FOX_HEADS // 2
    v_off = 2 * pairs
    og_off = 3 * pairs
    items = [(qi, kb) for qi in range(S // T) for kb in range(qi + 1)]
    q_tab = jnp.asarray([qi for qi, _ in items], jnp.int32)
    k_tab = jnp.asarray([kb for _, kb in items], jnp.int32)
    grid_spec = pltpu.PrefetchScalarGridSpec(
        num_scalar_prefetch=2,
        grid=(B, pairs),
        in_specs=[
            pl.BlockSpec((1, 2, S, LANES), lambda b, p, qt, kt: (b, p, 0, 0)),
            pl.BlockSpec((1, 2, S, LANES), lambda b, p, qt, kt: (b, p, 0, 0)),
            pl.BlockSpec((1, S, LANES), lambda b, p, qt, kt: (b, 0, v_off + p)),
            pl.BlockSpec((1, S, LANES), lambda b, p, qt, kt: (b, 0, og_off + p)),
        ],
        out_specs=pl.BlockSpec((1, S, LANES), lambda b, p, qt, kt: (b, 0, p)),
        scratch_shapes=[
            pltpu.VMEM((S, 2 * LANES), BF16),
            pltpu.VMEM((T, T), F32),
            pltpu.VMEM((2, T, T), F32), pltpu.VMEM((2, T, T), F32),
            pltpu.VMEM((2, T, T), BF16), pltpu.VMEM((2, T, T), BF16),
            pltpu.VMEM((2, T, LANES), F32), pltpu.VMEM((2, T, LANES), F32),
            pltpu.VMEM((2, T, LANES), F32),
            pltpu.VMEM((2, T, 2 * LANES), F32),
        ],
    )
    return pl.pallas_call(
        _fox_flash_kernel,
        grid_spec=grid_spec,
        out_shape=jax.ShapeDtypeStruct((B, S, D), BF16),
        compiler_params=_cparams(("parallel", "parallel")),
        name="fox_flash",
    )(q_tab, k_tab, qa, ka, qkvo, qkvo)


def _swiglu_acc(h, w1_ref, w3_ref, w2_ref, acc_ref):
    tf = w2_ref.shape[0]
    for c in range(0, tf, FF_SUB):
        e = min(c + FF_SUB, tf)
        a = _dot(h, w1_ref[:, c:e])
        b = _dot(h, w3_ref[:, c:e])
        act = (a * _sigmoid(a) * b).astype(BF16)
        acc_ref[...] += _dot(act, w2_ref[c:e, :])


def _dense_ffn_kernel(x_ref, y_ref, wo_ref, g1_ref, sc_ref, sh_ref, g_ref, w1_ref, w3_ref, w2_ref,
                      ea_ref, eb_ref, o_ref, ea_out_ref, eb_out_ref, x1_ref, acc_ref):
    x1_ref[...] = x_ref[0] + g1_ref[0] * _dot(y_ref[0], wo_ref[...])
    h = _norm_mod(x1_ref[...], sc_ref[0], sh_ref[0]).astype(BF16)
    acc_ref[...] = jnp.zeros_like(acc_ref)
    _swiglu_acc(h, w1_ref.at[0], w3_ref.at[0], w2_ref.at[0], acc_ref)
    o_ref[0] = x1_ref[...] + g_ref[0] * acc_ref[...]
    ea_out_ref[...] = ea_ref[0].astype(BF16)
    eb_out_ref[...] = eb_ref[0].astype(BF16)


def _dense_ffn(x, y, w_out, g1, sc, sh, g, w13, w2, layer, moe_w13, moe_w2, moe_layer):
    B, S, _ = x.shape
    steps = B * (S // ROW_TILE)
    n_l = moe_w13.shape[0]
    ea = moe_w13.reshape(n_l, N_EXPERTS * D, 2 * D_FF_EXPERT)
    eb = moe_w2.reshape(n_l, N_EXPERTS * D_FF_EXPERT, D)
    ra, rb = ea.shape[1] // steps, eb.shape[1] // steps
    assert ra * steps == ea.shape[1] and rb * steps == eb.shape[1] and ra % 16 == 0 and rb % 16 == 0
    step = lambda b, i: b * (S // ROW_TILE) + i
    once = pl.Buffered(1)
    return pl.pallas_call(
        _dense_ffn_kernel,
        grid=(B, S // ROW_TILE),
        in_specs=[_row_spec(D), _row_spec(D), pl.BlockSpec((D, D), lambda b, i: (0, 0), pipeline_mode=once),
                  _vec_spec(), _vec_spec(), _vec_spec(), _vec_spec(),
                  pl.BlockSpec((1, D, D_FF), lambda b, i: (layer, 0, 0), pipeline_mode=once),
                  pl.BlockSpec((1, D, D_FF), lambda b, i: (layer, 0, 1), pipeline_mode=once),
                  pl.BlockSpec((1, D_FF, D), lambda b, i: (layer, 0, 0), pipeline_mode=once),
                  pl.BlockSpec((1, ra, ea.shape[2]), lambda b, i: (moe_layer, step(b, i), 0)),
                  pl.BlockSpec((1, rb, D), lambda b, i: (moe_layer, step(b, i), 0))],
        out_specs=[_row_spec(D),
                   pl.BlockSpec((ra, ea.shape[2]), lambda b, i: (step(b, i), 0)),
                   pl.BlockSpec((rb, D), lambda b, i: (step(b, i), 0))],
        out_shape=[jax.ShapeDtypeStruct((B, S, D), F32),
                   jax.ShapeDtypeStruct(ea.shape[1:], BF16),
                   jax.ShapeDtypeStruct(eb.shape[1:], BF16)],
        scratch_shapes=[pltpu.VMEM((ROW_TILE, D), F32), pltpu.VMEM((ROW_TILE, D), F32)],
        compiler_params=_cparams(("arbitrary", "arbitrary")),
        name="dense_ffn",
    )(x, y, w_out, g1, sc, sh, g, w13, w13, w2, ea, eb)


def _moe_ffn_kernel(be_ref, nb_ref, x_ref, w1_ref, w3_ref, w2_ref, o_ref, acc_ref):
    i = pl.program_id(0)
    f = pl.program_id(1)

    @pl.when(f == 0)
    def _():
        acc_ref[...] = jnp.zeros_like(acc_ref)

    @pl.when(i < nb_ref[0])
    def _():
        _swiglu_acc(x_ref[...], w1_ref.at[0], w3_ref.at[0], w2_ref.at[0], acc_ref)

    @pl.when(f == pl.num_programs(1) - 1)
    def _():
        o_ref[...] = acc_ref[...].astype(BF16)


MOE_FF_TILE = 1792


def _moe_ffn(block_e, n_used, xg, w13, w2):
    P = xg.shape[0]
    tf = MOE_FF_TILE
    nf = D_FF_EXPERT // tf
    last = nf - 1

    def held(i, nb):
        return jnp.minimum(i, nb[0] - 1)

    def fcol(i, f, nb):
        return jnp.where(i < nb[0], f, last)

    grid_spec = pltpu.PrefetchScalarGridSpec(
        num_scalar_prefetch=2,
        grid=(P // MOE_TILE, nf),
        in_specs=[
            pl.BlockSpec((MOE_TILE, D), lambda i, f, be, nb: (held(i, nb), 0)),
            pl.BlockSpec((1, D, tf), lambda i, f, be, nb: (be[i], 0, fcol(i, f, nb))),
            pl.BlockSpec((1, D, tf), lambda i, f, be, nb: (be[i], 0, nf + fcol(i, f, nb))),
            pl.BlockSpec((1, tf, D), lambda i, f, be, nb: (be[i], fcol(i, f, nb), 0)),
        ],
        out_specs=pl.BlockSpec((MOE_TILE, D), lambda i, f, be, nb: (i, 0)),
        scratch_shapes=[pltpu.VMEM((MOE_TILE, D), F32)],
    )
    return pl.pallas_call(
        _moe_ffn_kernel,
        grid_spec=grid_spec,
        out_shape=jax.ShapeDtypeStruct((P, D), BF16),
        compiler_params=_cparams(("arbitrary", "arbitrary")),
        name="moe_ffn",
    )(block_e, n_used, xg, w13, w13, w2)


def _router_kernel(x_ref, sc_ref, sh_ref, wh_ref, wl_ref, h_ref, e_ref, g_ref):
    h = _norm_mod(x_ref[0], sc_ref[0], sh_ref[0])
    h_hi = h.astype(BF16)
    h_lo = (h - h_hi.astype(F32)).astype(BF16)
    h_ref[0] = h_hi
    lane = lax.broadcasted_iota(jnp.int32, (1, LANES), 1)
    lanef = lane.astype(F32)
    lg = _dot(h_hi, wh_ref[...]) + (_dot(h_lo, wh_ref[...]) + _dot(h_hi, wl_ref[...]))
    lg = jnp.where(lane < N_EXPERTS, lg, NEG_BIG)
    m1 = jnp.max(lg, axis=-1, keepdims=True)
    i1 = jnp.min(jnp.where(lg == m1, lanef, float(LANES)), axis=-1, keepdims=True)
    lg2 = jnp.where(lanef == i1, NEG_BIG, lg)
    m2 = jnp.max(lg2, axis=-1, keepdims=True)
    i2 = jnp.min(jnp.where(lg2 == m2, lanef, float(LANES)), axis=-1, keepdims=True)
    e = jnp.exp(m2 - m1)
    g1 = 1.0 / (1.0 + e)
    g2 = e / (1.0 + e)
    e_ref[0] = jnp.where(lane == 0, i1, jnp.where(lane == 1, i2, 0.0)).astype(jnp.int32)
    g_ref[0] = jnp.where(lane == 0, g1, jnp.where(lane == 1, g2, 0.0))


def _router(x, sc, sh, w_router):
    B, S, _ = x.shape
    w_hi = w_router.astype(BF16)
    w_lo = (w_router - w_hi.astype(F32)).astype(BF16)
    return pl.pallas_call(
        _router_kernel,
        grid=(B, S // ROW_TILE),
        in_specs=[_row_spec(D), _vec_spec(), _vec_spec(), _full_spec((D, LANES)), _full_spec((D, LANES))],
        out_specs=[_row_spec(D), _row_spec(LANES), _row_spec(LANES)],
        out_shape=[jax.ShapeDtypeStruct((B, S, D), BF16),
                   jax.ShapeDtypeStruct((B, S, LANES), jnp.int32),
                   jax.ShapeDtypeStruct((B, S, LANES), F32)],
        compiler_params=_cparams(("parallel", "parallel")),
        name="router",
    )(x, sc, sh, w_hi, w_lo)


def _combined(x_ref, g_ref, gate_ref, ya_ref, yb_ref):
    gate = gate_ref[0]
    y = ya_ref[0] * gate[:, 0:1] + yb_ref[0] * gate[:, 1:2]
    return x_ref[0] + g_ref[0] * y


def _combine_kernel(x_ref, g_ref, gate_ref, ya_ref, yb_ref, o_ref):
    o_ref[0] = _combined(x_ref, g_ref, gate_ref, ya_ref, yb_ref)


def _combine_norm_kernel(x_ref, g_ref, gate_ref, ya_ref, yb_ref, gf_ref, o_ref):
    x = _combined(x_ref, g_ref, gate_ref, ya_ref, yb_ref)
    ms = jnp.mean(x * x, axis=-1, keepdims=True)
    o_ref[0] = x * lax.rsqrt(ms + EPS) * gf_ref[...]


def _combine(x, g, gates, ya, yb, norm_g=None):
    B, S, _ = x.shape
    in_specs = [_row_spec(D), _vec_spec(), _row_spec(LANES), _row_spec(D), _row_spec(D)]
    args = [x, g, gates, ya, yb]
    body = _combine_kernel
    if norm_g is not None:
        in_specs.append(_full_spec((1, D)))
        args.append(norm_g.reshape(1, D))
        body = _combine_norm_kernel
    return pl.pallas_call(
        body,
        grid=(B, S // ROW_TILE),
        in_specs=in_specs,
        out_specs=_row_spec(D),
        out_shape=jax.ShapeDtypeStruct((B, S, D), F32),
        compiler_params=_cparams(("parallel", "parallel")),
        name="moe_combine",
    )(*args)


def _pad_cols(w, width):
    return jnp.pad(w, ((0, 0), (0, width - w.shape[1])))


def _gla_mixer(x, sc, sh, w_in, w_gate2, b_gate, gn_g):
    n_main = 2 * GLA_DK + 2 * GLA_DV
    w_main = w_in[:, :n_main].astype(BF16)
    w_glr = _pad_cols(w_in[:, n_main:], LANES).astype(BF16)
    w_g2 = jnp.pad(w_gate2, ((0, LANES - GLA_RANK), (0, 0))).astype(BF16)
    qkvr, log_a = _gla_proj(x, sc, sh, w_main, w_glr, w_g2, b_gate.reshape(1, GLA_DK))
    return _gla_scan(qkvr, log_a, gn_g)


def _conv_layer(x, sc, sh, g, w_in, b_in, dw, dw_b, ln_g, ln_b, w_out, b_out):
    u = _conv_proj(x, sc, sh, w_in.astype(BF16), b_in.reshape(1, 2 * D))
    return _conv_post(u, dw, dw_b.reshape(1, D), ln_g.reshape(1, D), ln_b.reshape(1, D),
                      w_out.astype(BF16), b_out.reshape(1, D), x, g)


def _fox_mixer(x, sc, sh, w_in, b_f, qn_g, kn_g):
    w_main = jnp.concatenate([w_in[:, :3 * D], w_in[:, 3 * D + FOX_HEADS:]], axis=1).astype(BF16)
    w_fl = _pad_cols(jnp.tile(w_in[:, 3 * D:3 * D + FOX_HEADS], (1, FOX_PART_COPIES)), LANES).astype(BF16)
    b_fl = _pad_cols(jnp.tile(b_f.reshape(1, FOX_HEADS), (1, FOX_PART_COPIES)), LANES)
    qkvo, lf = _fox_proj(x, sc, sh, w_main, w_fl, b_fl)
    cum = _fox_cum(lf)
    qa, ka = _fox_prep(qkvo, cum, qn_g, kn_g)
    return _fox_flash(qa, ka, qkvo)


def _moe_layer(x, sc, sh, g, w_router, w13, w2, norm_g):
    B, S, _ = x.shape
    w13 = w13.reshape(N_EXPERTS, D, 2 * D_FF_EXPERT)
    w2 = w2.reshape(N_EXPERTS, D_FF_EXPERT, D)
    N = B * S
    A = 2 * N
    h, e_pad, gates = _router(x, sc, sh, _pad_cols(w_router, LANES))
    e_flat = e_pad[:, :, :2].reshape(A)

    onehot = (e_flat[:, None] == jnp.arange(N_EXPERTS, dtype=jnp.int32)[None, :]).astype(jnp.int32)
    csum = jnp.cumsum(onehot, axis=0)
    rank = jnp.sum(csum * onehot, axis=1) - 1
    counts = csum[-1]
    padded = (counts + MOE_TILE - 1) // MOE_TILE * MOE_TILE
    pad_end = jnp.cumsum(padded)
    pad_start = pad_end - padded
    dest = pad_start[e_flat] + rank
    n_blocks = A // MOE_TILE + N_EXPERTS
    P = n_blocks * MOE_TILE
    block_e = jnp.minimum(
        jnp.searchsorted(pad_end, jnp.arange(n_blocks, dtype=jnp.int32) * MOE_TILE, side='right'),
        N_EXPERTS - 1).astype(jnp.int32)
    n_used = (pad_end[-1] // MOE_TILE).astype(jnp.int32).reshape(1)
    tok = jnp.arange(A, dtype=jnp.int32) // 2
    region_start = jnp.concatenate([pad_start + counts, pad_end[-1:]])
    pads_before = jnp.concatenate([jnp.zeros((1,), jnp.int32), jnp.cumsum(padded - counts)])
    j = jnp.arange(P - A, dtype=jnp.int32)[:, None]
    region = jnp.sum((j >= pads_before[None, 1:]).astype(jnp.int32), axis=1, keepdims=True)
    pick = (region == jnp.arange(N_EXPERTS + 1, dtype=jnp.int32)[None, :]).astype(jnp.int32)
    dummy = jnp.sum(pick * (region_start + j - pads_before), axis=1)
    keys = jnp.concatenate([dest, dummy])
    _, slot_tok = lax.sort_key_val(keys, jnp.concatenate([tok, jnp.zeros((P - A,), jnp.int32)]))

    xg = h.reshape(N, D).at[slot_tok].get(mode='promise_in_bounds')
    ys = _moe_ffn(block_e, n_used, xg, w13, w2)
    dest2 = dest.reshape(N, 2)
    ya = ys.at[dest2[:, 0]].get(mode='promise_in_bounds').reshape(B, S, D)
    yb = ys.at[dest2[:, 1]].get(mode='promise_in_bounds').reshape(B, S, D)
    return _combine(x, g, gates, ya, yb, norm_g)


def kernel(x, c, ada_w, ada_b, gla_w_in, gla_w_gate2, gla_b_gate, gla_gn_g, gla_w_out, conv_w_in, conv_b_in, conv_dw, conv_dw_b, conv_ln_g, conv_ln_b, conv_w_out, conv_b_out, fox_w_in, fox_b_f, fox_qn_g, fox_kn_g, fox_w_out, ffn_w13, ffn_w2, moe_router, moe_w13, moe_w2, norm_f_g):
    depth = ada_w.shape[0]
    B = x.shape[0]
    mod = _ada_mod(c, ada_w, ada_b).reshape(depth, 6, B, 1, D)
    assert depth % 2 == 0
    ffn_w13, ffn_w2 = _cast_bf16(ffn_w13), _cast_bf16(ffn_w2)
    expert_w = None
    for i in range(depth):
        sh1, sc1, g1, sh2, sc2, g2 = [mod[i, j] for j in range(6)]
        m, j = i % 3, i // 3
        dense = i % 2 == 0
        y = w_out = None
        if m == 0:
            y = _gla_mixer(x, sc1, sh1, gla_w_in[j], gla_w_gate2[j], gla_b_gate[j], gla_gn_g[j])
            w_out = gla_w_out[j].astype(BF16)
        elif m == 1:
            x = _conv_layer(x, sc1, sh1, g1, conv_w_in[j], conv_b_in[j], conv_dw[j], conv_dw_b[j],
                            conv_ln_g[j], conv_ln_b[j], conv_w_out[j], conv_b_out[j])
        else:
            y = _fox_mixer(x, sc1, sh1, fox_w_in[j], fox_b_f[j], fox_qn_g[j], fox_kn_g[j])
            w_out = fox_w_out[j].astype(BF16)
        if y is not None and not dense:
            x = _out_proj(y, w_out, jnp.zeros((1, D), F32), x, g1)
        if dense:
            assert y is not None
            x, *expert_w = _dense_ffn(x, y, w_out, g1, sc2, sh2, g2, ffn_w13, ffn_w2, i // 2,
                                      moe_w13, moe_w2, i // 2)
        else:
            x = _moe_layer(x, sc2, sh2, g2, moe_router[i // 2], *expert_w,
                           norm_f_g if i == depth - 1 else None)
    return x
```

```python
import math

import jax
import jax.numpy as jnp
from jax import lax
from jax.experimental import pallas as pl
from jax.experimental.pallas import tpu as pltpu

F32 = jnp.float32
BF16 = jnp.bfloat16
HIGHEST = lax.Precision.HIGHEST

D = 1024
EPS = 1e-6
CHUNK = 64
GLA_HEADS = 4
GLA_DK = 512
GLA_DV = 1024
GLA_DKH = 128
GLA_DVH = 256
GLA_RANK = 16
GLA_TAU = 16.0
CONV_WIDTH = 31
CONV_HALO = 32
FOX_HEADS = 16
FOX_DH = 64
N_EXPERTS = 8
D_FF = 2816
D_FF_EXPERT = 3584
LANES = 128
SUBLANES = 8
NEG_BIG = -1e30
LOG2E = 1.4426950408889634
VMEM_LIMIT = 56 * 1024 * 1024

ROW_TILE = 512
MOE_TILE = 512
FF_SUB = 256
CAST_BLOCK_ELEMS = 1 << 20


def _cparams(sem):
    return pltpu.CompilerParams(dimension_semantics=sem, vmem_limit_bytes=VMEM_LIMIT)


def _dot(a, b):
    return jnp.dot(a, b, preferred_element_type=F32)


def _dot_nt(a, b):
    return lax.dot_general(a, b, (((1,), (1,)), ((), ())), preferred_element_type=F32)


def _dot_tn(a, b):
    return lax.dot_general(a, b, (((0,), (0,)), ((), ())), preferred_element_type=F32)


def _sigmoid(x):
    return 1.0 / (1.0 + jnp.exp(-x))


def _log_sigmoid(z):
    return jnp.minimum(z, 0.0) - jnp.log(1.0 + jnp.exp(-jnp.abs(z)))


def _norm_mod(x, sc, sh):
    ms = jnp.mean(x * x, axis=-1, keepdims=True)
    return x * lax.rsqrt(ms + EPS) * (1.0 + sc) + sh


def _split3(x):
    hi = x.astype(BF16).astype(F32)
    r1 = x - hi
    mid = r1.astype(BF16).astype(F32)
    lo = (r1 - mid).astype(BF16).astype(F32)
    return hi, mid, lo


def _tri_cumsum(tri, x):
    hi, mid, lo = _split3(x)
    return _dot(tri, hi.astype(BF16)) + _dot(tri, mid.astype(BF16)) + _dot(tri, lo.astype(BF16))


def _lower_tri(n):
    r = lax.broadcasted_iota(jnp.int32, (n, n), 0)
    c = lax.broadcasted_iota(jnp.int32, (n, n), 1)
    return jnp.where(r >= c, 1.0, 0.0).astype(BF16)


def _cast_kernel(x_ref, o_ref):
    o_ref[...] = x_ref[...].astype(BF16)


def _cast_bf16(w):
    cols = w.shape[-1]
    rows = math.prod(w.shape[:-1])
    block = math.gcd(rows, 1 << (CAST_BLOCK_ELEMS // cols).bit_length() - 1)
    spec = pl.BlockSpec((block, cols), lambda i: (i, 0))
    out = pl.pallas_call(
        _cast_kernel,
        grid=(rows // block,),
        in_specs=[spec],
        out_specs=spec,
        out_shape=jax.ShapeDtypeStruct((rows, cols), BF16),
        compiler_params=_cparams(("parallel",)),
        name="cast_bf16",
    )(w.reshape(rows, cols))
    return out.reshape(w.shape)


def _ada_kernel(c_ref, w_ref, b_ref, o_ref):
    c = c_ref[...]
    cond = c * _sigmoid(c)
    o_ref[0, 0] = jnp.dot(cond, w_ref[0], precision=HIGHEST,
                          preferred_element_type=F32) + b_ref[0, 0]


def _ada_mod(c, ada_w, ada_b):
    depth = ada_w.shape[0]
    B = c.shape[0]
    return pl.pallas_call(
        _ada_kernel,
        grid=(depth, 6),
        in_specs=[
            pl.BlockSpec((B, D), lambda l, j: (0, 0)),
            pl.BlockSpec((1, D, D), lambda l, j: (l, 0, j)),
            pl.BlockSpec((1, 1, 1, D), lambda l, j: (l, j, 0, 0)),
        ],
        out_specs=pl.BlockSpec((1, 1, B, D), lambda l, j: (l, j, 0, 0)),
        out_shape=jax.ShapeDtypeStruct((depth, 6, B, D), F32),
        compiler_params=_cparams(("arbitrary", "arbitrary")),
        name="ada_mod",
    )(c, ada_w, ada_b.reshape(depth, 6, 1, D))


def _row_spec(width, col=0):
    return pl.BlockSpec((1, ROW_TILE, width), lambda b, i: (b, i, col))


def _vec_spec(width=D):
    return pl.BlockSpec((1, 1, width), lambda b, i: (b, 0, 0))


def _full_spec(shape):
    return pl.BlockSpec(shape, lambda b, i: (0,) * len(shape))


def _gla_proj_kernel(x_ref, sc_ref, sh_ref, w_ref, wg_ref, w2_ref, bg_ref, o_ref, la_ref):
    h = _norm_mod(x_ref[0], sc_ref[0], sh_ref[0]).astype(BF16)
    n_out = o_ref.shape[2]
    for c in range(0, n_out, 512):
        o_ref[0, :, c:c + 512] = _dot(h, w_ref[:, c:c + 512]).astype(BF16)
    glr = _dot(h, wg_ref[...]).astype(BF16)
    z = _dot(glr, w2_ref[...]) + bg_ref[...]
    la_ref[0] = _log_sigmoid(z) * (1.0 / GLA_TAU)


def _gla_proj(x, sc, sh, w_main, w_glr, w_g2, b_gate):
    B, S, _ = x.shape
    n_out = w_main.shape[1]
    return pl.pallas_call(
        _gla_proj_kernel,
        grid=(B, S // ROW_TILE),
        in_specs=[_row_spec(D), _vec_spec(), _vec_spec(),
                  _full_spec((D, n_out)), _full_spec((D, LANES)),
                  _full_spec((LANES, GLA_DK)), _full_spec((1, GLA_DK))],
        out_specs=[_row_spec(n_out), _row_spec(GLA_DK)],
        out_shape=[jax.ShapeDtypeStruct((B, S, n_out), BF16),
                   jax.ShapeDtypeStruct((B, S, GLA_DK), F32)],
        compiler_params=_cparams(("parallel", "parallel")),
        name="gla_proj",
    )(x, sc, sh, w_main, w_glr, w_g2, b_gate)


def _conv_proj_kernel(x_ref, sc_ref, sh_ref, w_ref, b_ref, o_ref):
    h = _norm_mod(x_ref[0], sc_ref[0], sh_ref[0]).astype(BF16)
    for c in range(0, D, 512):
        a = _dot(h, w_ref[:, c:c + 512]) + b_ref[:, c:c + 512]
        g = _dot(h, w_ref[:, D + c:D + c + 512]) + b_ref[:, D + c:D + c + 512]
        o_ref[0, :, c:c + 512] = (a * _sigmoid(g)).astype(BF16)


def _conv_proj(x, sc, sh, w_in, b_in):
    B, S, _ = x.shape
    return pl.pallas_call(
        _conv_proj_kernel,
        grid=(B, S // ROW_TILE),
        in_specs=[_row_spec(D), _vec_spec(), _vec_spec(),
                  _full_spec((D, 2 * D)), _full_spec((1, 2 * D))],
        out_specs=_row_spec(D),
        out_shape=jax.ShapeDtypeStruct((B, S, D), BF16),
        compiler_params=_cparams(("parallel", "parallel")),
        name="conv_proj",
    )(x, sc, sh, w_in, b_in)


def _fox_proj_kernel(x_ref, sc_ref, sh_ref, w_ref, wf_ref, bf_ref, o_ref, lf_ref):
    h = _norm_mod(x_ref[0], sc_ref[0], sh_ref[0]).astype(BF16)
    n_out = o_ref.shape[2]
    for c in range(0, n_out, 512):
        o_ref[0, :, c:c + 512] = _dot(h, w_ref[:, c:c + 512]).astype(BF16)
    lf_ref[0] = _log_sigmoid(_dot(h, wf_ref[...]) + bf_ref[...])


def _fox_proj(x, sc, sh, w_main, w_fl, b_f):
    B, S, _ = x.shape
    n_out = w_main.shape[1]
    return pl.pallas_call(
        _fox_proj_kernel,
        grid=(B, S // ROW_TILE),
        in_specs=[_row_spec(D), _vec_spec(), _vec_spec(),
                  _full_spec((D, n_out)), _full_spec((D, LANES)), _full_spec((1, LANES))],
        out_specs=[_row_spec(n_out), _row_spec(LANES)],
        out_shape=[jax.ShapeDtypeStruct((B, S, n_out), BF16),
                   jax.ShapeDtypeStruct((B, S, LANES), F32)],
        compiler_params=_cparams(("parallel", "parallel")),
        name="fox_proj",
    )(x, sc, sh, w_main, w_fl, b_f)


def _out_proj_kernel(y_ref, w_ref, b_ref, x_ref, g_ref, o_ref):
    for c in range(0, D, 512):
        y = _dot(y_ref[0], w_ref[:, c:c + 512]) + b_ref[:, c:c + 512]
        o_ref[0, :, c:c + 512] = x_ref[0, :, c:c + 512] + g_ref[0][:, c:c + 512] * y


def _out_proj(y, w, b, x, g):
    B, S, _ = x.shape
    return pl.pallas_call(
        _out_proj_kernel,
        grid=(B, S // ROW_TILE),
        in_specs=[_row_spec(D), _full_spec((D, D)), _full_spec((1, D)), _row_spec(D), _vec_spec()],
        out_specs=_row_spec(D),
        out_shape=jax.ShapeDtypeStruct((B, S, D), F32),
        compiler_params=_cparams(("parallel", "parallel")),
        name="out_proj",
    )(y, w, b, x, g)


def _gla_scan_kernel(q_ref, k_ref, v_ref, r_ref, la_ref, gn_ref, o_ref, st_ref):
    @pl.when(pl.program_id(1) == 0)
    def _():
        st_ref[...] = jnp.zeros_like(st_ref)

    tri = _lower_tri(CHUNK)
    scale = GLA_DKH ** -0.5
    states = [st_ref[h] for h in range(GLA_HEADS)]
    for c in range(ROW_TILE // CHUNK):
        rows = slice(c * CHUNK, (c + 1) * CHUNK)
        cum = _tri_cumsum(tri, la_ref[0, rows, :])
        tot = cum[CHUNK - 1:CHUNK, :]
        kd = (k_ref[0, rows, :].astype(F32) * jnp.exp(tot - cum)).astype(BF16)
        decay = jnp.exp(tot)
        for h in range(GLA_HEADS):
            ks = slice(h * GLA_DKH, (h + 1) * GLA_DKH)
            vs = slice(h * GLA_DVH, (h + 1) * GLA_DVH)
            st = states[h] * decay[:, ks] + _dot_tn(v_ref[0, rows, vs], kd[:, ks])
            states[h] = st
            o = _dot_nt(q_ref[0, rows, ks], st.astype(BF16)) * scale
            ms = jnp.mean(o * o, axis=-1, keepdims=True)
            on = o * lax.rsqrt(ms + EPS) * gn_ref[:, vs]
            r = r_ref[0, rows, vs].astype(F32)
            o_ref[0, rows, vs] = (on * (r * _sigmoid(r))).astype(BF16)
    for h in range(GLA_HEADS):
        st_ref[h] = states[h]


def _gla_scan(qkvr, log_a, gn_g):
    B, S, _ = qkvr.shape
    return pl.pallas_call(
        _gla_scan_kernel,
        grid=(B, S // ROW_TILE),
        in_specs=[_row_spec(GLA_DK, 0), _row_spec(GLA_DK, 1), _row_spec(GLA_DV, 1), _row_spec(GLA_DV, 2),
                  _row_spec(GLA_DK, 0), _full_spec((1, GLA_DV))],
        out_specs=_row_spec(GLA_DV),
        out_shape=jax.ShapeDtypeStruct((B, S, GLA_DV), BF16),
        scratch_shapes=[pltpu.VMEM((GLA_HEADS, GLA_DVH, GLA_DKH), F32)],
        compiler_params=_cparams(("parallel", "arbitrary")),
        name="gla_scan",
    )(qkvr, qkvr, qkvr, qkvr, log_a, gn_g.reshape(1, GLA_DV))


CONV_ROWS = 128
CONV_COLS = 256


def _conv_post_kernel(uc_ref, up_ref, dw_ref, dwb_ref, lng_ref, lnb_ref, w_ref, bo_ref,
                      x_ref, g_ref, o_ref, ext_ref, sh_ref, cv_ref):
    first = pl.program_id(1) == 0
    halo = up_ref[0].astype(F32)
    ext_ref[0:CONV_HALO, :] = jnp.where(first, 0.0, halo)
    ext_ref[CONV_HALO:, :] = uc_ref[0].astype(F32)
    lead = CONV_HALO - (CONV_WIDTH - 1)
    for j in range(1, SUBLANES):
        sh_ref[j - 1] = ext_ref[pl.ds(j, sh_ref.shape[1]), :]

    for c in range(0, D, CONV_COLS):
        def row_group(r, carry, c=c):
            r0 = pl.multiple_of(r * CONV_ROWS, CONV_ROWS)
            acc = jnp.zeros((CONV_ROWS, CONV_COLS), F32) + dwb_ref[:, c:c + CONV_COLS]
            for w in range(CONV_WIDTH):
                j = (lead + w) % SUBLANES
                rows = pl.ds(pl.multiple_of(r0 + (lead + w - j), SUBLANES), CONV_ROWS)
                src = ext_ref if j == 0 else sh_ref.at[j - 1]
                acc = acc + dw_ref[w:w + 1, c:c + CONV_COLS] * src[rows, c:c + CONV_COLS]
            cv_ref[pl.ds(r0, CONV_ROWS), c:c + CONV_COLS] = acc
            return carry
        lax.fori_loop(0, ROW_TILE // CONV_ROWS, row_group, 0)

    u = cv_ref[...]
    mu = jnp.mean(u, axis=-1, keepdims=True)
    var = jnp.mean(jnp.square(u - mu), axis=-1, keepdims=True)
    un = (u - mu) * lax.rsqrt(var + EPS) * lng_ref[...] + lnb_ref[...]
    y = (un * _sigmoid(un)).astype(BF16)
    for c in range(0, D, 512):
        z = _dot(y, w_ref[:, c:c + 512]) + bo_ref[:, c:c + 512]
        o_ref[0, :, c:c + 512] = x_ref[0, :, c:c + 512] + g_ref[0][:, c:c + 512] * z


def _conv_post(u, dw, dw_b, ln_g, ln_b, w_out, b_out, x, g):
    B, S, _ = x.shape
    halo_blocks = ROW_TILE // CONV_HALO
    return pl.pallas_call(
        _conv_post_kernel,
        grid=(B, S // ROW_TILE),
        in_specs=[
            _row_spec(D),
            pl.BlockSpec((1, CONV_HALO, D), lambda b, i: (b, jnp.maximum(i * halo_blocks - 1, 0), 0)),
            _full_spec((CONV_WIDTH, D)), _full_spec((1, D)), _full_spec((1, D)), _full_spec((1, D)),
            _full_spec((D, D)), _full_spec((1, D)), _row_spec(D), _vec_spec(),
        ],
        out_specs=_row_spec(D),
        out_shape=jax.ShapeDtypeStruct((B, S, D), F32),
        scratch_shapes=[pltpu.VMEM((ROW_TILE + CONV_HALO, D), F32),
                        pltpu.VMEM((SUBLANES - 1, ROW_TILE + CONV_HALO - SUBLANES, D), F32),
                        pltpu.VMEM((ROW_TILE, D), F32)],
        compiler_params=_cparams(("parallel", "parallel")),
        name="conv_post",
    )(u, u, dw, dw_b, ln_g, ln_b, w_out, b_out, x, g)


CUM_TILE = 256
PREP_TILE = 2048
FOX_PART_COPIES = 3


def _fox_cum_kernel(lf_ref, o_ref, carry_ref):
    @pl.when(pl.program_id(1) == 0)
    def _():
        carry_ref[...] = jnp.zeros_like(carry_ref)

    cum = _tri_cumsum(_lower_tri(CUM_TILE), lf_ref[0]) + carry_ref[...]
    o_ref[0] = cum
    carry_ref[...] = cum[CUM_TILE - 1:CUM_TILE, :]


def _fox_cum(lf):
    B, S, _ = lf.shape
    spec = pl.BlockSpec((1, CUM_TILE, LANES), lambda b, i: (b, i, 0))
    return pl.pallas_call(
        _fox_cum_kernel,
        grid=(B, S // CUM_TILE),
        in_specs=[spec],
        out_specs=spec,
        out_shape=jax.ShapeDtypeStruct((B, S, LANES), F32),
        scratch_shapes=[pltpu.VMEM((1, LANES), F32)],
        compiler_params=_cparams(("parallel", "arbitrary")),
        name="fox_cum",
    )(lf)


def _fox_prep_kernel(q_ref, k_ref, cum_ref, qg_ref, kg_ref, qa_ref, ka_ref):
    pair = pl.program_id(1)
    lane = lax.broadcasted_iota(jnp.int32, (1, LANES), 1)
    r2 = lax.broadcasted_iota(jnp.int32, (2 * LANES, LANES), 0)
    c2 = lax.broadcasted_iota(jnp.int32, (2 * LANES, LANES), 1)
    same_head = ((r2 & (LANES - 1)) >> 6) == (c2 >> 6)
    head_sum = jnp.where(same_head, 1.0, 0.0).astype(BF16)

    def head_norm(t_ref, g_ref, scale):
        t = t_ref[0].astype(F32)
        sq = t * t
        hi = sq.astype(BF16)
        mid = (sq - hi.astype(F32)).astype(BF16)
        ss = _dot(jnp.concatenate([hi, mid], axis=1), head_sum)
        return (t * lax.rsqrt(ss * (1.0 / FOX_DH) + EPS) * (g_ref[...] * scale)).astype(BF16)

    qb = head_norm(q_ref, qg_ref, LOG2E * FOX_DH ** -0.5)
    kb = head_norm(k_ref, kg_ref, 1.0)
    c_hi, c_mid, c_lo = _split3(cum_ref[0] * LOG2E)
    parts = jnp.where(lane < FOX_HEADS, c_hi,
            jnp.where(lane < 2 * FOX_HEADS, c_mid,
            jnp.where(lane < 3 * FOX_HEADS, c_lo, 0.0))).astype(BF16)
    lhs_q = jnp.concatenate([qb, parts], axis=1)
    lhs_k = jnp.concatenate([kb, parts], axis=1)
    d = FOX_DH
    for hh in range(2):
        h = 2 * pair + hh
        part_row = LANES + h
        src_q = jnp.where(lane < d, lane + hh * d,
                jnp.where(lane < d + 3, part_row + FOX_HEADS * (lane - d), -1))
        src_k = jnp.where(lane < d, lane + hh * d,
                jnp.where((lane >= d + 3) & (lane < d + 6), part_row + FOX_HEADS * (lane - d - 3), -1))
        wq = jnp.where(r2 == src_q, 1.0, 0.0).astype(BF16)
        wk = jnp.where(r2 == src_k, jnp.where(lane < d, 1.0, -1.0), 0.0).astype(BF16)
        ones_q = jnp.where((lane >= d + 3) & (lane < d + 6), 1.0, 0.0)
        ones_k = jnp.where((lane >= d) & (lane < d + 3), 1.0, 0.0)
        qa_ref[0, hh] = (_dot(lhs_q, wq) + ones_q).astype(BF16)
        ka_ref[0, hh] = (_dot(lhs_k, wk) + ones_k).astype(BF16)


def _fox_prep(qkvo, cum, qn_g, kn_g):
    B, S, _ = qkvo.shape
    pairs = FOX_HEADS // 2
    g2 = lambda g: jnp.concatenate([g, g]).reshape(1, LANES)
    tile = min(S, PREP_TILE)
    aug = pl.BlockSpec((1, 2, tile, LANES), lambda b, p, i: (b, p, i, 0))
    return pl.pallas_call(
        _fox_prep_kernel,
        grid=(B, pairs, S // tile),
        in_specs=[
            pl.BlockSpec((1, tile, LANES), lambda b, p, i: (b, i, p)),
            pl.BlockSpec((1, tile, LANES), lambda b, p, i: (b, i, pairs + p)),
            pl.BlockSpec((1, tile, LANES), lambda b, p, i: (b, i, 0)),
            pl.BlockSpec((1, LANES), lambda b, p, i: (0, 0)),
            pl.BlockSpec((1, LANES), lambda b, p, i: (0, 0)),
        ],
        out_specs=[aug, aug],
        out_shape=[jax.ShapeDtypeStruct((B, FOX_HEADS, S, LANES), BF16)] * 2,
        compiler_params=_cparams(("parallel", "parallel", "parallel")),
        name="fox_prep",
    )(qkvo, qkvo, cum, g2(qn_g), g2(kn_g))


ATT_TILE = 512


def _fox_flash_kernel(qt_ref, kt_ref, qa_ref, ka_ref, v_ref, og_ref, o_ref,
                      vx_ref, cap_ref, s0_ref, s1_ref, p0_ref, p1_ref, al0_ref, al1_ref, m_ref, acc_ref):
    T = ATT_TILE
    n_items = qt_ref.shape[0]
    lane = lax.broadcasted_iota(jnp.int32, (1, LANES), 1)
    s_refs, p_refs, al_refs = (s0_ref, s1_ref), (p0_ref, p1_ref), (al0_ref, al1_ref)

    vx_ref[:, :LANES] = v_ref[0]
    ones_col = jnp.where(lane == 0, 1.0, 0.0).astype(BF16)
    vx_ref[:, LANES:] = jnp.broadcast_to(ones_col, (vx_ref.shape[0], LANES))
    causal = (lax.broadcasted_iota(jnp.int32, (T, T), 1) <= lax.broadcasted_iota(jnp.int32, (T, T), 0))
    cap_ref[...] = jnp.where(causal, -NEG_BIG, NEG_BIG)
    for slot in range(2):
        s_refs[slot][...] = jnp.zeros_like(s_refs[slot])
        p_refs[slot][...] = jnp.zeros_like(p_refs[slot])
        al_refs[slot][...] = jnp.ones_like(al_refs[slot])
    m_ref[...] = jnp.full(m_ref.shape, NEG_BIG, F32)
    acc_ref[...] = jnp.zeros_like(acc_ref)

    def block(i):
        return pl.ds(pl.multiple_of(i * T, T), T)

    def tick(t, slot):
        it = jnp.minimum(t, n_items - 1)
        q1, k1 = qt_ref[it], kt_ref[it]
        for hh in range(2):
            s_refs[slot][hh] = _dot_nt(qa_ref[0, hh, block(q1), :], ka_ref[0, hh, block(k1), :])
        it = jnp.clip(t - 1, 0, n_items - 1)
        live = (t >= 1) & (t <= n_items)
        fresh = kt_ref[it] == 0
        for hh in range(2):
            s = s_refs[1 - slot][hh]
            m_old = jnp.where(fresh, NEG_BIG, m_ref[hh])
            m_new = jnp.maximum(m_old, jnp.max(s, axis=-1, keepdims=True))
            m_ref[hh] = jnp.where(live, m_new, m_ref[hh])
            al_refs[1 - slot][hh] = jnp.where(live, jnp.exp2(m_old - m_new), 1.0)
            m_use = jnp.where(live, m_new, -NEG_BIG)
            p_refs[1 - slot][hh] = jnp.exp2(s - pltpu.repeat(m_use, T // LANES, axis=1)).astype(BF16)
        it = jnp.clip(t - 2, 0, n_items - 1)
        q3, k3 = qt_ref[it], kt_ref[it]
        for hh in range(2):
            alpha = pltpu.repeat(al_refs[slot][hh], 2, axis=1)
            acc_ref[hh] = alpha * acc_ref[hh] + _dot(p_refs[slot][hh], vx_ref[block(k3), :])

        @pl.when(q1 == k1)
        def _():
            for hh in range(2):
                s_refs[slot][hh] = jnp.minimum(s_refs[slot][hh], cap_ref[...])

        @pl.when((t >= 2) & (t <= n_items + 1) & (q3 == k3))
        def _():
            heads = [acc_ref[hh, :, :LANES] / acc_ref[hh, :, LANES:LANES + 1] for hh in range(2)]
            o = jnp.where(lane < FOX_DH, heads[0], heads[1])
            gate = _sigmoid(og_ref[0, block(q3), :].astype(F32))
            o_ref[0, block(q3), :] = (o * gate).astype(BF16)

    def two_ticks(i, carry):
        tick(2 * i, 0)
        tick(2 * i + 1, 1)
        return carry

    lax.fori_loop(0, (n_items + 3) // 2, two_ticks, 0)


def _fox_flash(qa, ka, qkvo):
    B, _, S, _ = qa.shape
    T = ATT_TILE
    pairs = FOX_HEADS // 2
    v_off = 2 * pairs
    og_off = 3 * pairs
    items = [(qi, kb) for qi in range(S // T) for kb in range(qi + 1)]
    q_tab = jnp.asarray([qi for qi, _ in items], jnp.int32)
    k_tab = jnp.asarray([kb for _, kb in items], jnp.int32)
    grid_spec = pltpu.PrefetchScalarGridSpec(
        num_scalar_prefetch=2,
        grid=(B, pairs),
        in_specs=[
            pl.BlockSpec((1, 2, S, LANES), lambda b, p, qt, kt: (b, p, 0, 0)),
            pl.BlockSpec((1, 2, S, LANES), lambda b, p, qt, kt: (b, p, 0, 0)),
            pl.BlockSpec((1, S, LANES), lambda b, p, qt, kt: (b, 0, v_off + p)),
            pl.BlockSpec((1, S, LANES), lambda b, p, qt, kt: (b, 0, og_off + p)),
        ],
        out_specs=pl.BlockSpec((1, S, LANES), lambda b, p, qt, kt: (b, 0, p)),
        scratch_shapes=[
            pltpu.VMEM((S, 2 * LANES), BF16),
            pltpu.VMEM((T, T), F32),
            pltpu.VMEM((2, T, T), F32), pltpu.VMEM((2, T, T), F32),
            pltpu.VMEM((2, T, T), BF16), pltpu.VMEM((2, T, T), BF16),
            pltpu.VMEM((2, T, LANES), F32), pltpu.VMEM((2, T, LANES), F32),
            pltpu.VMEM((2, T, LANES), F32),
            pltpu.VMEM((2, T, 2 * LANES), F32),
        ],
    )
    return pl.pallas_call(
        _fox_flash_kernel,
        grid_spec=grid_spec,
        out_shape=jax.ShapeDtypeStruct((B, S, D), BF16),
        compiler_params=_cparams(("parallel", "parallel")),
        name="fox_flash",
    )(q_tab, k_tab, qa, ka, qkvo, qkvo)


def _swiglu_acc(h, w1_ref, w3_ref, w2_ref, acc_ref):
    tf = w2_ref.shape[0]
    for c in range(0, tf, FF_SUB):
        e = min(c + FF_SUB, tf)
        a = _dot(h, w1_ref[:, c:e])
        b = _dot(h, w3_ref[:, c:e])
        act = (a * _sigmoid(a) * b).astype(BF16)
        acc_ref[...] += _dot(act, w2_ref[c:e, :])


def _dense_ffn_kernel(x_ref, y_ref, wo_ref, g1_ref, sc_ref, sh_ref, g_ref, w1_ref, w3_ref, w2_ref,
                      ea_ref, eb_ref, o_ref, ea_out_ref, eb_out_ref, x1_ref, acc_ref):
    x1_ref[...] = x_ref[0] + g1_ref[0] * _dot(y_ref[0], wo_ref[...])
    h = _norm_mod(x1_ref[...], sc_ref[0], sh_ref[0]).astype(BF16)
    acc_ref[...] = jnp.zeros_like(acc_ref)
    _swiglu_acc(h, w1_ref.at[0], w3_ref.at[0], w2_ref.at[0], acc_ref)
    o_ref[0] = x1_ref[...] + g_ref[0] * acc_ref[...]
    ea_out_ref[...] = ea_ref[0].astype(BF16)
    eb_out_ref[...] = eb_ref[0].astype(BF16)


def _dense_ffn(x, y, w_out, g1, sc, sh, g, w13, w2, layer, moe_w13, moe_w2, moe_layer):
    B, S, _ = x.shape
    steps = B * (S // ROW_TILE)
    n_l = moe_w13.shape[0]
    ea = moe_w13.reshape(n_l, N_EXPERTS * D, 2 * D_FF_EXPERT)
    eb = moe_w2.reshape(n_l, N_EXPERTS * D_FF_EXPERT, D)
    ra, rb = ea.shape[1] // steps, eb.shape[1] // steps
    assert ra * steps == ea.shape[1] and rb * steps == eb.shape[1] and ra % 16 == 0 and rb % 16 == 0
    step = lambda b, i: b * (S // ROW_TILE) + i
    once = pl.Buffered(1)
    return pl.pallas_call(
        _dense_ffn_kernel,
        grid=(B, S // ROW_TILE),
        in_specs=[_row_spec(D), _row_spec(D), pl.BlockSpec((D, D), lambda b, i: (0, 0), pipeline_mode=once),
                  _vec_spec(), _vec_spec(), _vec_spec(), _vec_spec(),
                  pl.BlockSpec((1, D, D_FF), lambda b, i: (layer, 0, 0), pipeline_mode=once),
                  pl.BlockSpec((1, D, D_FF), lambda b, i: (layer, 0, 1), pipeline_mode=once),
                  pl.BlockSpec((1, D_FF, D), lambda b, i: (layer, 0, 0), pipeline_mode=once),
                  pl.BlockSpec((1, ra, ea.shape[2]), lambda b, i: (moe_layer, step(b, i), 0)),
                  pl.BlockSpec((1, rb, D), lambda b, i: (moe_layer, step(b, i), 0))],
        out_specs=[_row_spec(D),
                   pl.BlockSpec((ra, ea.shape[2]), lambda b, i: (step(b, i), 0)),
                   pl.BlockSpec((rb, D), lambda b, i: (step(b, i), 0))],
        out_shape=[jax.ShapeDtypeStruct((B, S, D), F32),
                   jax.ShapeDtypeStruct(ea.shape[1:], BF16),
                   jax.ShapeDtypeStruct(eb.shape[1:], BF16)],
        scratch_shapes=[pltpu.VMEM((ROW_TILE, D), F32), pltpu.VMEM((ROW_TILE, D), F32)],
        compiler_params=_cparams(("arbitrary", "arbitrary")),
        name="dense_ffn",
    )(x, y, w_out, g1, sc, sh, g, w13, w13, w2, ea, eb)


def _moe_ffn_kernel(be_ref, nb_ref, x_ref, w1_ref, w3_ref, w2_ref, o_ref, acc_ref):
    i = pl.program_id(0)
    f = pl.program_id(1)

    @pl.when(f == 0)
    def _():
        acc_ref[...] = jnp.zeros_like(acc_ref)

    @pl.when(i < nb_ref[0])
    def _():
        _swiglu_acc(x_ref[...], w1_ref.at[0], w3_ref.at[0], w2_ref.at[0], acc_ref)

    @pl.when(f == pl.num_programs(1) - 1)
    def _():
        o_ref[...] = acc_ref[...].astype(BF16)


MOE_FF_TILE = 1792


def _moe_ffn(block_e, n_used, xg, w13, w2):
    P = xg.shape[0]
    tf = MOE_FF_TILE
    nf = D_FF_EXPERT // tf
    last = nf - 1

    def held(i, nb):
        return jnp.minimum(i, nb[0] - 1)

    def fcol(i, f, nb):
        return jnp.where(i < nb[0], f, last)

    grid_spec = pltpu.PrefetchScalarGridSpec(
        num_scalar_prefetch=2,
        grid=(P // MOE_TILE, nf),
        in_specs=[
            pl.BlockSpec((MOE_TILE, D), lambda i, f, be, nb: (held(i, nb), 0)),
            pl.BlockSpec((1, D, tf), lambda i, f, be, nb: (be[i], 0, fcol(i, f, nb))),
            pl.BlockSpec((1, D, tf), lambda i, f, be, nb: (be[i], 0, nf + fcol(i, f, nb))),
            pl.BlockSpec((1, tf, D), lambda i, f, be, nb: (be[i], fcol(i, f, nb), 0)),
        ],
        out_specs=pl.BlockSpec((MOE_TILE, D), lambda i, f, be, nb: (i, 0)),
        scratch_shapes=[pltpu.VMEM((MOE_TILE, D), F32)],
    )
    return pl.pallas_call(
        _moe_ffn_kernel,
        grid_spec=grid_spec,
        out_shape=jax.ShapeDtypeStruct((P, D), BF16),
        compiler_params=_cparams(("arbitrary", "arbitrary")),
        name="moe_ffn",
    )(block_e, n_used, xg, w13, w13, w2)


def _router_kernel(x_ref, sc_ref, sh_ref, wh_ref, wl_ref, h_ref, e_ref, g_ref):
    h = _norm_mod(x_ref[0], sc_ref[0], sh_ref[0])
    h_hi = h.astype(BF16)
    h_lo = (h - h_hi.astype(F32)).astype(BF16)
    h_ref[0] = h_hi
    lane = lax.broadcasted_iota(jnp.int32, (1, LANES), 1)
    lanef = lane.astype(F32)
    lg = _dot(h_hi, wh_ref[...]) + (_dot(h_lo, wh_ref[...]) + _dot(h_hi, wl_ref[...]))
    lg = jnp.where(lane < N_EXPERTS, lg, NEG_BIG)
    m1 = jnp.max(lg, axis=-1, keepdims=True)
    i1 = jnp.min(jnp.where(lg == m1, lanef, float(LANES)), axis=-1, keepdims=True)
    lg2 = jnp.where(lanef == i1, NEG_BIG, lg)
    m2 = jnp.max(lg2, axis=-1, keepdims=True)
    i2 = jnp.min(jnp.where(lg2 == m2, lanef, float(LANES)), axis=-1, keepdims=True)
    e = jnp.exp(m2 - m1)
    g1 = 1.0 / (1.0 + e)
    g2 = e / (1.0 + e)
    e_ref[0] = jnp.where(lane == 0, i1, jnp.where(lane == 1, i2, 0.0)).astype(jnp.int32)
    g_ref[0] = jnp.where(lane == 0, g1, jnp.where(lane == 1, g2, 0.0))


def _router(x, sc, sh, w_router):
    B, S, _ = x.shape
    w_hi = w_router.astype(BF16)
    w_lo = (w_router - w_hi.astype(F32)).astype(BF16)
    return pl.pallas_call(
        _router_kernel,
        grid=(B, S // ROW_TILE),
        in_specs=[_row_spec(D), _vec_spec(), _vec_spec(), _full_spec((D, LANES)), _full_spec((D, LANES))],
        out_specs=[_row_spec(D), _row_spec(LANES), _row_spec(LANES)],
        out_shape=[jax.ShapeDtypeStruct((B, S, D), BF16),
                   jax.ShapeDtypeStruct((B, S, LANES), jnp.int32),
                   jax.ShapeDtypeStruct((B, S, LANES), F32)],
        compiler_params=_cparams(("parallel", "parallel")),
        name="router",
    )(x, sc, sh, w_hi, w_lo)


def _combined(x_ref, g_ref, gate_ref, ya_ref, yb_ref):
    gate = gate_ref[0]
    y = ya_ref[0] * gate[:, 0:1] + yb_ref[0] * gate[:, 1:2]
    return x_ref[0] + g_ref[0] * y


def _combine_kernel(x_ref, g_ref, gate_ref, ya_ref, yb_ref, o_ref):
    o_ref[0] = _combined(x_ref, g_ref, gate_ref, ya_ref, yb_ref)


def _combine_norm_kernel(x_ref, g_ref, gate_ref, ya_ref, yb_ref, gf_ref, o_ref):
    x = _combined(x_ref, g_ref, gate_ref, ya_ref, yb_ref)
    ms = jnp.mean(x * x, axis=-1, keepdims=True)
    o_ref[0] = x * lax.rsqrt(ms + EPS) * gf_ref[...]


def _combine(x, g, gates, y_pair, norm_g=None):
    B, S, _ = x.shape
    in_specs = [_row_spec(D), _vec_spec(), _row_spec(LANES), _row_spec(D, 0), _row_spec(D, 1)]
    args = [x, g, gates, y_pair, y_pair]
    body = _combine_kernel
    if norm_g is not None:
        in_specs.append(_full_spec((1, D)))
        args.append(norm_g.reshape(1, D))
        body = _combine_norm_kernel
    return pl.pallas_call(
        body,
        grid=(B, S // ROW_TILE),
        in_specs=in_specs,
        out_specs=_row_spec(D),
        out_shape=jax.ShapeDtypeStruct((B, S, D), F32),
        compiler_params=_cparams(("parallel", "parallel")),
        name="moe_combine",
    )(*args)


def _pad_cols(w, width):
    return jnp.pad(w, ((0, 0), (0, width - w.shape[1])))


def _gla_mixer(x, sc, sh, w_in, w_gate2, b_gate, gn_g):
    n_main = 2 * GLA_DK + 2 * GLA_DV
    w_main = w_in[:, :n_main].astype(BF16)
    w_glr = _pad_cols(w_in[:, n_main:], LANES).astype(BF16)
    w_g2 = jnp.pad(w_gate2, ((0, LANES - GLA_RANK), (0, 0))).astype(BF16)
    qkvr, log_a = _gla_proj(x, sc, sh, w_main, w_glr, w_g2, b_gate.reshape(1, GLA_DK))
    return _gla_scan(qkvr, log_a, gn_g)


def _conv_layer(x, sc, sh, g, w_in, b_in, dw, dw_b, ln_g, ln_b, w_out, b_out):
    u = _conv_proj(x, sc, sh, w_in.astype(BF16), b_in.reshape(1, 2 * D))
    return _conv_post(u, dw, dw_b.reshape(1, D), ln_g.reshape(1, D), ln_b.reshape(1, D),
                      w_out.astype(BF16), b_out.reshape(1, D), x, g)


def _fox_mixer(x, sc, sh, w_in, b_f, qn_g, kn_g):
    w_main = jnp.concatenate([w_in[:, :3 * D], w_in[:, 3 * D + FOX_HEADS:]], axis=1).astype(BF16)
    w_fl = _pad_cols(jnp.tile(w_in[:, 3 * D:3 * D + FOX_HEADS], (1, FOX_PART_COPIES)), LANES).astype(BF16)
    b_fl = _pad_cols(jnp.tile(b_f.reshape(1, FOX_HEADS), (1, FOX_PART_COPIES)), LANES)
    qkvo, lf = _fox_proj(x, sc, sh, w_main, w_fl, b_fl)
    cum = _fox_cum(lf)
    qa, ka = _fox_prep(qkvo, cum, qn_g, kn_g)
    return _fox_flash(qa, ka, qkvo)


def _moe_layer(x, sc, sh, g, w_router, w13, w2, norm_g):
    B, S, _ = x.shape
    w13 = w13.reshape(N_EXPERTS, D, 2 * D_FF_EXPERT)
    w2 = w2.reshape(N_EXPERTS, D_FF_EXPERT, D)
    N = B * S
    A = 2 * N
    h, e_pad, gates = _router(x, sc, sh, _pad_cols(w_router, LANES))
    e_flat = e_pad[:, :, :2].reshape(A)

    onehot = (e_flat[:, None] == jnp.arange(N_EXPERTS, dtype=jnp.int32)[None, :]).astype(jnp.int32)
    csum = jnp.cumsum(onehot, axis=0)
    rank = jnp.sum(csum * onehot, axis=1) - 1
    counts = csum[-1]
    padded = (counts + MOE_TILE - 1) // MOE_TILE * MOE_TILE
    pad_end = jnp.cumsum(padded)
    pad_start = pad_end - padded
    dest = pad_start[e_flat] + rank
    n_blocks = A // MOE_TILE + N_EXPERTS
    P = n_blocks * MOE_TILE
    block_e = jnp.minimum(
        jnp.searchsorted(pad_end, jnp.arange(n_blocks, dtype=jnp.int32) * MOE_TILE, side='right'),
        N_EXPERTS - 1).astype(jnp.int32)
    n_used = (pad_end[-1] // MOE_TILE).astype(jnp.int32).reshape(1)
    tok = jnp.arange(A, dtype=jnp.int32) // 2
    region_start = jnp.concatenate([pad_start + counts, pad_end[-1:]])
    pads_before = jnp.concatenate([jnp.zeros((1,), jnp.int32), jnp.cumsum(padded - counts)])
    j = jnp.arange(P - A, dtype=jnp.int32)[:, None]
    region = jnp.sum((j >= pads_before[None, 1:]).astype(jnp.int32), axis=1, keepdims=True)
    pick = (region == jnp.arange(N_EXPERTS + 1, dtype=jnp.int32)[None, :]).astype(jnp.int32)
    dummy = jnp.sum(pick * (region_start + j - pads_before), axis=1)
    keys = jnp.concatenate([dest, dummy])
    pad_tok = jnp.arange(P - A, dtype=jnp.int32) % N
    _, slot_tok = lax.sort_key_val(keys, jnp.concatenate([tok, pad_tok]))

    xg = h.reshape(N, D).at[slot_tok].get(mode='promise_in_bounds')
    ys = _moe_ffn(block_e, n_used, xg, w13, w2)
    y_pair = ys.at[dest].get(mode='promise_in_bounds').reshape(B, S, 2 * D)
    return _combine(x, g, gates, y_pair, norm_g)


def kernel(x, c, ada_w, ada_b, gla_w_in, gla_w_gate2, gla_b_gate, gla_gn_g, gla_w_out, conv_w_in, conv_b_in, conv_dw, conv_dw_b, conv_ln_g, conv_ln_b, conv_w_out, conv_b_out, fox_w_in, fox_b_f, fox_qn_g, fox_kn_g, fox_w_out, ffn_w13, ffn_w2, moe_router, moe_w13, moe_w2, norm_f_g):
    depth = ada_w.shape[0]
    B = x.shape[0]
    mod = _ada_mod(c, ada_w, ada_b).reshape(depth, 6, B, 1, D)
    assert depth % 2 == 0
    ffn_w13, ffn_w2 = _cast_bf16(ffn_w13), _cast_bf16(ffn_w2)
    expert_w = None
    for i in range(depth):
        sh1, sc1, g1, sh2, sc2, g2 = [mod[i, j] for j in range(6)]
        m, j = i % 3, i // 3
        dense = i % 2 == 0
        y = w_out = None
        if m == 0:
            y = _gla_mixer(x, sc1, sh1, gla_w_in[j], gla_w_gate2[j], gla_b_gate[j], gla_gn_g[j])
            w_out = gla_w_out[j].astype(BF16)
        elif m == 1:
            x = _conv_layer(x, sc1, sh1, g1, conv_w_in[j], conv_b_in[j], conv_dw[j], conv_dw_b[j],
                            conv_ln_g[j], conv_ln_b[j], conv_w_out[j], conv_b_out[j])
        else:
            y = _fox_mixer(x, sc1, sh1, fox_w_in[j], fox_b_f[j], fox_qn_g[j], fox_kn_g[j])
            w_out = fox_w_out[j].astype(BF16)
        if y is not None and not dense:
            x = _out_proj(y, w_out, jnp.zeros((1, D), F32), x, g1)
        if dense:
            assert y is not None
            x, *expert_w = _dense_ffn(x, y, w_out, g1, sc2, sh2, g2, ffn_w13, ffn_w2, i // 2,
                                      moe_w13, moe_w2, i // 2)
        else:
            x = _moe_layer(x, sc2, sh2, g2, moe_router[i // 2], *expert_w,
                           norm_f_g if i == depth - 1 else None)
    return x
```

```python
import math

import jax
import jax.numpy as jnp
from jax import lax
from jax.experimental import pallas as pl
from jax.experimental.pallas import tpu as pltpu

F32 = jnp.float32
BF16 = jnp.bfloat16
HIGHEST = lax.Precision.HIGHEST

D = 1024
EPS = 1e-6
CHUNK = 64
GLA_HEADS = 4
GLA_DK = 512
GLA_DV = 1024
GLA_DKH = 128
GLA_DVH = 256
GLA_RANK = 16
GLA_TAU = 16.0
CONV_WIDTH = 31
CONV_HALO = 32
FOX_HEADS = 16
FOX_DH = 64
N_EXPERTS = 8
D_FF = 2816
D_FF_EXPERT = 3584
LANES = 128
SUBLANES = 8
NEG_BIG = -1e30
LOG2E = 1.4426950408889634
VMEM_LIMIT = 56 * 1024 * 1024

ROW_TILE = 512
MOE_TILE = 512
FF_SUB = 256
CAST_BLOCK_ELEMS = 1 << 20


def _cparams(sem):
    return pltpu.CompilerParams(dimension_semantics=sem, vmem_limit_bytes=VMEM_LIMIT)


def _dot(a, b):
    return jnp.dot(a, b, preferred_element_type=F32)


def _dot_nt(a, b):
    return lax.dot_general(a, b, (((1,), (1,)), ((), ())), preferred_element_type=F32)


def _dot_tn(a, b):
    return lax.dot_general(a, b, (((0,), (0,)), ((), ())), preferred_element_type=F32)


def _sigmoid(x):
    return 1.0 / (1.0 + jnp.exp(-x))


def _log_sigmoid(z):
    return jnp.minimum(z, 0.0) - jnp.log(1.0 + jnp.exp(-jnp.abs(z)))


def _norm_mod(x, sc, sh):
    ms = jnp.mean(x * x, axis=-1, keepdims=True)
    return x * lax.rsqrt(ms + EPS) * (1.0 + sc) + sh


def _split3(x):
    hi = x.astype(BF16).astype(F32)
    r1 = x - hi
    mid = r1.astype(BF16).astype(F32)
    lo = (r1 - mid).astype(BF16).astype(F32)
    return hi, mid, lo


def _tri_cumsum(tri, x):
    hi, mid, lo = _split3(x)
    return _dot(tri, hi.astype(BF16)) + _dot(tri, mid.astype(BF16)) + _dot(tri, lo.astype(BF16))


def _lower_tri(n):
    r = lax.broadcasted_iota(jnp.int32, (n, n), 0)
    c = lax.broadcasted_iota(jnp.int32, (n, n), 1)
    return jnp.where(r >= c, 1.0, 0.0).astype(BF16)


def _cast_kernel(x_ref, o_ref):
    o_ref[...] = x_ref[...].astype(BF16)


def _cast_bf16(w):
    cols = w.shape[-1]
    rows = math.prod(w.shape[:-1])
    block = math.gcd(rows, 1 << (CAST_BLOCK_ELEMS // cols).bit_length() - 1)
    spec = pl.BlockSpec((block, cols), lambda i: (i, 0))
    out = pl.pallas_call(
        _cast_kernel,
        grid=(rows // block,),
        in_specs=[spec],
        out_specs=spec,
        out_shape=jax.ShapeDtypeStruct((rows, cols), BF16),
        compiler_params=_cparams(("parallel",)),
        name="cast_bf16",
    )(w.reshape(rows, cols))
    return out.reshape(w.shape)


def _ada_kernel(c_ref, w_ref, b_ref, o_ref):
    c = c_ref[...]
    cond = c * _sigmoid(c)
    o_ref[0, 0] = jnp.dot(cond, w_ref[0], precision=HIGHEST,
                          preferred_element_type=F32) + b_ref[0, 0]


def _ada_mod(c, ada_w, ada_b):
    depth = ada_w.shape[0]
    B = c.shape[0]
    return pl.pallas_call(
        _ada_kernel,
        grid=(depth, 6),
        in_specs=[
            pl.BlockSpec((B, D), lambda l, j: (0, 0)),
            pl.BlockSpec((1, D, D), lambda l, j: (l, 0, j)),
            pl.BlockSpec((1, 1, 1, D), lambda l, j: (l, j, 0, 0)),
        ],
        out_specs=pl.BlockSpec((1, 1, B, D), lambda l, j: (l, j, 0, 0)),
        out_shape=jax.ShapeDtypeStruct((depth, 6, B, D), F32),
        compiler_params=_cparams(("arbitrary", "arbitrary")),
        name="ada_mod",
    )(c, ada_w, ada_b.reshape(depth, 6, 1, D))


def _row_spec(width, col=0):
    return pl.BlockSpec((1, ROW_TILE, width), lambda b, i: (b, i, col))


def _vec_spec(width=D):
    return pl.BlockSpec((1, 1, width), lambda b, i: (b, 0, 0))


def _full_spec(shape):
    return pl.BlockSpec(shape, lambda b, i: (0,) * len(shape))


def _gla_proj_kernel(x_ref, sc_ref, sh_ref, w_ref, wg_ref, w2_ref, bg_ref, o_ref, la_ref):
    h = _norm_mod(x_ref[0], sc_ref[0], sh_ref[0]).astype(BF16)
    n_out = o_ref.shape[2]
    for c in range(0, n_out, 512):
        o_ref[0, :, c:c + 512] = _dot(h, w_ref[:, c:c + 512]).astype(BF16)
    glr = _dot(h, wg_ref[...]).astype(BF16)
    z = _dot(glr, w2_ref[...]) + bg_ref[...]
    la_ref[0] = _log_sigmoid(z) * (1.0 / GLA_TAU)


def _gla_proj(x, sc, sh, w_main, w_glr, w_g2, b_gate):
    B, S, _ = x.shape
    n_out = w_main.shape[1]
    return pl.pallas_call(
        _gla_proj_kernel,
        grid=(B, S // ROW_TILE),
        in_specs=[_row_spec(D), _vec_spec(), _vec_spec(),
                  _full_spec((D, n_out)), _full_spec((D, LANES)),
                  _full_spec((LANES, GLA_DK)), _full_spec((1, GLA_DK))],
        out_specs=[_row_spec(n_out), _row_spec(GLA_DK)],
        out_shape=[jax.ShapeDtypeStruct((B, S, n_out), BF16),
                   jax.ShapeDtypeStruct((B, S, GLA_DK), F32)],
        compiler_params=_cparams(("parallel", "parallel")),
        name="gla_proj",
    )(x, sc, sh, w_main, w_glr, w_g2, b_gate)


def _conv_proj_kernel(x_ref, sc_ref, sh_ref, w_ref, b_ref, o_ref):
    h = _norm_mod(x_ref[0], sc_ref[0], sh_ref[0]).astype(BF16)
    for c in range(0, D, 512):
        a = _dot(h, w_ref[:, c:c + 512]) + b_ref[:, c:c + 512]
        g = _dot(h, w_ref[:, D + c:D + c + 512]) + b_ref[:, D + c:D + c + 512]
        o_ref[0, :, c:c + 512] = (a * _sigmoid(g)).astype(BF16)


def _conv_proj(x, sc, sh, w_in, b_in):
    B, S, _ = x.shape
    return pl.pallas_call(
        _conv_proj_kernel,
        grid=(B, S // ROW_TILE),
        in_specs=[_row_spec(D), _vec_spec(), _vec_spec(),
                  _full_spec((D, 2 * D)), _full_spec((1, 2 * D))],
        out_specs=_row_spec(D),
        out_shape=jax.ShapeDtypeStruct((B, S, D), BF16),
        compiler_params=_cparams(("parallel", "parallel")),
        name="conv_proj",
    )(x, sc, sh, w_in, b_in)


def _fox_proj_kernel(x_ref, sc_ref, sh_ref, w_ref, wf_ref, bf_ref, o_ref, lf_ref):
    h = _norm_mod(x_ref[0], sc_ref[0], sh_ref[0]).astype(BF16)
    n_out = o_ref.shape[2]
    for c in range(0, n_out, 512):
        o_ref[0, :, c:c + 512] = _dot(h, w_ref[:, c:c + 512]).astype(BF16)
    lf_ref[0] = _log_sigmoid(_dot(h, wf_ref[...]) + bf_ref[...])


def _fox_proj(x, sc, sh, w_main, w_fl, b_f):
    B, S, _ = x.shape
    n_out = w_main.shape[1]
    return pl.pallas_call(
        _fox_proj_kernel,
        grid=(B, S // ROW_TILE),
        in_specs=[_row_spec(D), _vec_spec(), _vec_spec(),
                  _full_spec((D, n_out)), _full_spec((D, LANES)), _full_spec((1, LANES))],
        out_specs=[_row_spec(n_out), _row_spec(LANES)],
        out_shape=[jax.ShapeDtypeStruct((B, S, n_out), BF16),
                   jax.ShapeDtypeStruct((B, S, LANES), F32)],
        compiler_params=_cparams(("parallel", "parallel")),
        name="fox_proj",
    )(x, sc, sh, w_main, w_fl, b_f)


def _out_proj_kernel(y_ref, w_ref, b_ref, x_ref, g_ref, o_ref):
    for c in range(0, D, 512):
        y = _dot(y_ref[0], w_ref[:, c:c + 512]) + b_ref[:, c:c + 512]
        o_ref[0, :, c:c + 512] = x_ref[0, :, c:c + 512] + g_ref[0][:, c:c + 512] * y


def _out_proj(y, w, b, x, g):
    B, S, _ = x.shape
    return pl.pallas_call(
        _out_proj_kernel,
        grid=(B, S // ROW_TILE),
        in_specs=[_row_spec(D), _full_spec((D, D)), _full_spec((1, D)), _row_spec(D), _vec_spec()],
        out_specs=_row_spec(D),
        out_shape=jax.ShapeDtypeStruct((B, S, D), F32),
        compiler_params=_cparams(("parallel", "parallel")),
        name="out_proj",
    )(y, w, b, x, g)


def _gla_scan_kernel(q_ref, k_ref, v_ref, r_ref, la_ref, gn_ref, o_ref, st_ref):
    @pl.when(pl.program_id(1) == 0)
    def _():
        st_ref[...] = jnp.zeros_like(st_ref)

    tri = _lower_tri(CHUNK)
    scale = GLA_DKH ** -0.5
    states = [st_ref[h] for h in range(GLA_HEADS)]
    for c in range(ROW_TILE // CHUNK):
        rows = slice(c * CHUNK, (c + 1) * CHUNK)
        cum = _tri_cumsum(tri, la_ref[0, rows, :])
        tot = cum[CHUNK - 1:CHUNK, :]
        kd = (k_ref[0, rows, :].astype(F32) * jnp.exp(tot - cum)).astype(BF16)
        decay = jnp.exp(tot)
        for h in range(GLA_HEADS):
            ks = slice(h * GLA_DKH, (h + 1) * GLA_DKH)
            vs = slice(h * GLA_DVH, (h + 1) * GLA_DVH)
            st = states[h] * decay[:, ks] + _dot_tn(v_ref[0, rows, vs], kd[:, ks])
            states[h] = st
            o = _dot_nt(q_ref[0, rows, ks], st.astype(BF16)) * scale
            ms = jnp.mean(o * o, axis=-1, keepdims=True)
            on = o * lax.rsqrt(ms + EPS) * gn_ref[:, vs]
            r = r_ref[0, rows, vs].astype(F32)
            o_ref[0, rows, vs] = (on * (r * _sigmoid(r))).astype(BF16)
    for h in range(GLA_HEADS):
        st_ref[h] = states[h]


def _gla_scan(qkvr, log_a, gn_g):
    B, S, _ = qkvr.shape
    return pl.pallas_call(
        _gla_scan_kernel,
        grid=(B, S // ROW_TILE),
        in_specs=[_row_spec(GLA_DK, 0), _row_spec(GLA_DK, 1), _row_spec(GLA_DV, 1), _row_spec(GLA_DV, 2),
                  _row_spec(GLA_DK, 0), _full_spec((1, GLA_DV))],
        out_specs=_row_spec(GLA_DV),
        out_shape=jax.ShapeDtypeStruct((B, S, GLA_DV), BF16),
        scratch_shapes=[pltpu.VMEM((GLA_HEADS, GLA_DVH, GLA_DKH), F32)],
        compiler_params=_cparams(("parallel", "arbitrary")),
        name="gla_scan",
    )(qkvr, qkvr, qkvr, qkvr, log_a, gn_g.reshape(1, GLA_DV))


CONV_ROWS = 128
CONV_COLS = 256


def _conv_post_kernel(uc_ref, up_ref, dw_ref, dwb_ref, lng_ref, lnb_ref, w_ref, bo_ref,
                      x_ref, g_ref, o_ref, ext_ref, sh_ref, cv_ref):
    first = pl.program_id(1) == 0
    halo = up_ref[0].astype(F32)
    ext_ref[0:CONV_HALO, :] = jnp.where(first, 0.0, halo)
    ext_ref[CONV_HALO:, :] = uc_ref[0].astype(F32)
    lead = CONV_HALO - (CONV_WIDTH - 1)
    for j in range(1, SUBLANES):
        sh_ref[j - 1] = ext_ref[pl.ds(j, sh_ref.shape[1]), :]

    for c in range(0, D, CONV_COLS):
        def row_group(r, carry, c=c):
            r0 = pl.multiple_of(r * CONV_ROWS, CONV_ROWS)
            acc = jnp.zeros((CONV_ROWS, CONV_COLS), F32) + dwb_ref[:, c:c + CONV_COLS]
            for w in range(CONV_WIDTH):
                j = (lead + w) % SUBLANES
                rows = pl.ds(pl.multiple_of(r0 + (lead + w - j), SUBLANES), CONV_ROWS)
                src = ext_ref if j == 0 else sh_ref.at[j - 1]
                acc = acc + dw_ref[w:w + 1, c:c + CONV_COLS] * src[rows, c:c + CONV_COLS]
            cv_ref[pl.ds(r0, CONV_ROWS), c:c + CONV_COLS] = acc
            return carry
        lax.fori_loop(0, ROW_TILE // CONV_ROWS, row_group, 0)

    u = cv_ref[...]
    mu = jnp.mean(u, axis=-1, keepdims=True)
    var = jnp.mean(jnp.square(u - mu), axis=-1, keepdims=True)
    un = (u - mu) * lax.rsqrt(var + EPS) * lng_ref[...] + lnb_ref[...]
    y = (un * _sigmoid(un)).astype(BF16)
    for c in range(0, D, 512):
        z = _dot(y, w_ref[:, c:c + 512]) + bo_ref[:, c:c + 512]
        o_ref[0, :, c:c + 512] = x_ref[0, :, c:c + 512] + g_ref[0][:, c:c + 512] * z


def _conv_post(u, dw, dw_b, ln_g, ln_b, w_out, b_out, x, g):
    B, S, _ = x.shape
    halo_blocks = ROW_TILE // CONV_HALO
    return pl.pallas_call(
        _conv_post_kernel,
        grid=(B, S // ROW_TILE),
        in_specs=[
            _row_spec(D),
            pl.BlockSpec((1, CONV_HALO, D), lambda b, i: (b, jnp.maximum(i * halo_blocks - 1, 0), 0)),
            _full_spec((CONV_WIDTH, D)), _full_spec((1, D)), _full_spec((1, D)), _full_spec((1, D)),
            _full_spec((D, D)), _full_spec((1, D)), _row_spec(D), _vec_spec(),
        ],
        out_specs=_row_spec(D),
        out_shape=jax.ShapeDtypeStruct((B, S, D), F32),
        scratch_shapes=[pltpu.VMEM((ROW_TILE + CONV_HALO, D), F32),
                        pltpu.VMEM((SUBLANES - 1, ROW_TILE + CONV_HALO - SUBLANES, D), F32),
                        pltpu.VMEM((ROW_TILE, D), F32)],
        compiler_params=_cparams(("parallel", "parallel")),
        name="conv_post",
    )(u, u, dw, dw_b, ln_g, ln_b, w_out, b_out, x, g)


CUM_TILE = 256
PREP_TILE = 2048
FOX_PART_COPIES = 3


def _fox_cum_kernel(lf_ref, o_ref, carry_ref):
    @pl.when(pl.program_id(1) == 0)
    def _():
        carry_ref[...] = jnp.zeros_like(carry_ref)

    cum = _tri_cumsum(_lower_tri(CUM_TILE), lf_ref[0]) + carry_ref[...]
    o_ref[0] = cum
    carry_ref[...] = cum[CUM_TILE - 1:CUM_TILE, :]


def _fox_cum(lf):
    B, S, _ = lf.shape
    spec = pl.BlockSpec((1, CUM_TILE, LANES), lambda b, i: (b, i, 0))
    return pl.pallas_call(
        _fox_cum_kernel,
        grid=(B, S // CUM_TILE),
        in_specs=[spec],
        out_specs=spec,
        out_shape=jax.ShapeDtypeStruct((B, S, LANES), F32),
        scratch_shapes=[pltpu.VMEM((1, LANES), F32)],
        compiler_params=_cparams(("parallel", "arbitrary")),
        name="fox_cum",
    )(lf)


def _fox_prep_kernel(q_ref, k_ref, cum_ref, qg_ref, kg_ref, qa_ref, ka_ref):
    pair = pl.program_id(1)
    lane = lax.broadcasted_iota(jnp.int32, (1, LANES), 1)
    r2 = lax.broadcasted_iota(jnp.int32, (2 * LANES, LANES), 0)
    c2 = lax.broadcasted_iota(jnp.int32, (2 * LANES, LANES), 1)
    same_head = ((r2 & (LANES - 1)) >> 6) == (c2 >> 6)
    head_sum = jnp.where(same_head, 1.0, 0.0).astype(BF16)

    def head_norm(t_ref, g_ref, scale):
        t = t_ref[0].astype(F32)
        sq = t * t
        hi = sq.astype(BF16)
        mid = (sq - hi.astype(F32)).astype(BF16)
        ss = _dot(jnp.concatenate([hi, mid], axis=1), head_sum)
        return (t * lax.rsqrt(ss * (1.0 / FOX_DH) + EPS) * (g_ref[...] * scale)).astype(BF16)

    qb = head_norm(q_ref, qg_ref, LOG2E * FOX_DH ** -0.5)
    kb = head_norm(k_ref, kg_ref, 1.0)
    c_hi, c_mid, c_lo = _split3(cum_ref[0] * LOG2E)
    parts = jnp.where(lane < FOX_HEADS, c_hi,
            jnp.where(lane < 2 * FOX_HEADS, c_mid,
            jnp.where(lane < 3 * FOX_HEADS, c_lo, 0.0))).astype(BF16)
    lhs_q = jnp.concatenate([qb, parts], axis=1)
    lhs_k = jnp.concatenate([kb, parts], axis=1)
    d = FOX_DH
    for hh in range(2):
        h = 2 * pair + hh
        part_row = LANES + h
        src_q = jnp.where(lane < d, lane + hh * d,
                jnp.where(lane < d + 3, part_row + FOX_HEADS * (lane - d), -1))
        src_k = jnp.where(lane < d, lane + hh * d,
                jnp.where((lane >= d + 3) & (lane < d + 6), part_row + FOX_HEADS * (lane - d - 3), -1))
        wq = jnp.where(r2 == src_q, 1.0, 0.0).astype(BF16)
        wk = jnp.where(r2 == src_k, jnp.where(lane < d, 1.0, -1.0), 0.0).astype(BF16)
        ones_q = jnp.where((lane >= d + 3) & (lane < d + 6), 1.0, 0.0)
        ones_k = jnp.where((lane >= d) & (lane < d + 3), 1.0, 0.0)
        qa_ref[0, hh] = (_dot(lhs_q, wq) + ones_q).astype(BF16)
        ka_ref[0, hh] = (_dot(lhs_k, wk) + ones_k).astype(BF16)


def _fox_prep(qkvo, cum, qn_g, kn_g):
    B, S, _ = qkvo.shape
    pairs = FOX_HEADS // 2
    g2 = lambda g: jnp.concatenate([g, g]).reshape(1, LANES)
    tile = min(S, PREP_TILE)
    aug = pl.BlockSpec((1, 2, tile, LANES), lambda b, p, i: (b, p, i, 0))
    return pl.pallas_call(
        _fox_prep_kernel,
        grid=(B, pairs, S // tile),
        in_specs=[
            pl.BlockSpec((1, tile, LANES), lambda b, p, i: (b, i, p)),
            pl.BlockSpec((1, tile, LANES), lambda b, p, i: (b, i, pairs + p)),
            pl.BlockSpec((1, tile, LANES), lambda b, p, i: (b, i, 0)),
            pl.BlockSpec((1, LANES), lambda b, p, i: (0, 0)),
            pl.BlockSpec((1, LANES), lambda b, p, i: (0, 0)),
        ],
        out_specs=[aug, aug],
        out_shape=[jax.ShapeDtypeStruct((B, FOX_HEADS, S, LANES), BF16)] * 2,
        compiler_params=_cparams(("parallel", "parallel", "parallel")),
        name="fox_prep",
    )(qkvo, qkvo, cum, g2(qn_g), g2(kn_g))


ATT_TILE = 512


def _fox_flash_kernel(qt_ref, kt_ref, qa_ref, ka_ref, v_ref, og_ref, o_ref,
                      vx_ref, cap_ref, s0_ref, s1_ref, p0_ref, p1_ref, al0_ref, al1_ref, m_ref, acc_ref):
    T = ATT_TILE
    n_items = qt_ref.shape[0]
    lane = lax.broadcasted_iota(jnp.int32, (1, LANES), 1)
    s_refs, p_refs, al_refs = (s0_ref, s1_ref), (p0_ref, p1_ref), (al0_ref, al1_ref)

    vx_ref[:, :LANES] = v_ref[0]
    ones_col = jnp.where(lane == 0, 1.0, 0.0).astype(BF16)
    vx_ref[:, LANES:] = jnp.broadcast_to(ones_col, (vx_ref.shape[0], LANES))
    causal = (lax.broadcasted_iota(jnp.int32, (T, T), 1) <= lax.broadcasted_iota(jnp.int32, (T, T), 0))
    cap_ref[...] = jnp.where(causal, -NEG_BIG, NEG_BIG)
    for slot in range(2):
        s_refs[slot][...] = jnp.zeros_like(s_refs[slot])
        p_refs[slot][...] = jnp.zeros_like(p_refs[slot])
        al_refs[slot][...] = jnp.ones_like(al_refs[slot])
    m_ref[...] = jnp.full(m_ref.shape, NEG_BIG, F32)
    acc_ref[...] = jnp.zeros_like(acc_ref)

    def block(i):
        return pl.ds(pl.multiple_of(i * T, T), T)

    def tick(t, slot):
        it = jnp.minimum(t, n_items - 1)
        q1, k1 = qt_ref[it], kt_ref[it]
        for hh in range(2):
            s_refs[slot][hh] = _dot_nt(qa_ref[0, hh, block(q1), :], ka_ref[0, hh, block(k1), :])
        it = jnp.clip(t - 1, 0, n_items - 1)
        live = (t >= 1) & (t <= n_items)
        fresh = kt_ref[it] == 0
        for hh in range(2):
            s = s_refs[1 - slot][hh]
            m_old = jnp.where(fresh, NEG_BIG, m_ref[hh])
            m_new = jnp.maximum(m_old, jnp.max(s, axis=-1, keepdims=True))
            m_ref[hh] = jnp.where(live, m_new, m_ref[hh])
            al_refs[1 - slot][hh] = jnp.where(live, jnp.exp2(m_old - m_new), 1.0)
            m_use = jnp.where(live, m_new, -NEG_BIG)
            p_refs[1 - slot][hh] = jnp.exp2(s - pltpu.repeat(m_use, T // LANES, axis=1)).astype(BF16)
        it = jnp.clip(t - 2, 0, n_items - 1)
        q3, k3 = qt_ref[it], kt_ref[it]
        for hh in range(2):
            alpha = pltpu.repeat(al_refs[slot][hh], 2, axis=1)
            acc_ref[hh] = alpha * acc_ref[hh] + _dot(p_refs[slot][hh], vx_ref[block(k3), :])

        @pl.when(q1 == k1)
        def _():
            for hh in range(2):
                s_refs[slot][hh] = jnp.minimum(s_refs[slot][hh], cap_ref[...])

        @pl.when((t >= 2) & (t <= n_items + 1) & (q3 == k3))
        def _():
            heads = [acc_ref[hh, :, :LANES] / acc_ref[hh, :, LANES:LANES + 1] for hh in range(2)]
            o = jnp.where(lane < FOX_DH, heads[0], heads[1])
            gate = _sigmoid(og_ref[0, block(q3), :].astype(F32))
            o_ref[0, block(q3), :] = (o * gate).astype(BF16)

    def two_ticks(i, carry):
        tick(2 * i, 0)
        tick(2 * i + 1, 1)
        return carry

    lax.fori_loop(0, (n_items + 3) // 2, two_ticks, 0)


def _fox_flash(qa, ka, qkvo):
    B, _, S, _ = qa.shape
    T = ATT_TILE
    pairs = FOX_HEADS // 2
    v_off = 2 * pairs
    og_off = 3 * pairs
    items = [(qi, kb) for qi in range(S // T) for kb in range(qi + 1)]
    q_tab = jnp.asarray([qi for qi, _ in items], jnp.int32)
    k_tab = jnp.asarray([kb for _, kb in items], jnp.int32)
    grid_spec = pltpu.PrefetchScalarGridSpec(
        num_scalar_prefetch=2,
        grid=(B, pairs),
        in_specs=[
            pl.BlockSpec((1, 2, S, LANES), lambda b, p, qt, kt: (b, p, 0, 0)),
            pl.BlockSpec((1, 2, S, LANES), lambda b, p, qt, kt: (b, p, 0, 0)),
            pl.BlockSpec((1, S, LANES), lambda b, p, qt, kt: (b, 0, v_off + p)),
            pl.BlockSpec((1, S, LANES), lambda b, p, qt, kt: (b, 0, og_off + p)),
        ],
        out_specs=pl.BlockSpec((1, S, LANES), lambda b, p, qt, kt: (b, 0, p)),
        scratch_shapes=[
            pltpu.VMEM((S, 2 * LANES), BF16),
            pltpu.VMEM((T, T), F32),
            pltpu.VMEM((2, T, T), F32), pltpu.VMEM((2, T, T), F32),
            pltpu.VMEM((2, T, T), BF16), pltpu.VMEM((2, T, T), BF16),
            pltpu.VMEM((2, T, LANES), F32), pltpu.VMEM((2, T, LANES), F32),
            pltpu.VMEM((2, T, LANES), F32),
            pltpu.VMEM((2, T, 2 * LANES), F32),
        ],
    )
    return pl.pallas_call(
        _fox_flash_kernel,
        grid_spec=grid_spec,
        out_shape=jax.ShapeDtypeStruct((B, S, D), BF16),
        compiler_params=_cparams(("parallel", "parallel")),
        name="fox_flash",
    )(q_tab, k_tab, qa, ka, qkvo, qkvo)


def _swiglu_acc(h, w1_ref, w3_ref, w2_ref, acc_ref):
    tf = w2_ref.shape[0]
    for c in range(0, tf, FF_SUB):
        e = min(c + FF_SUB, tf)
        a = _dot(h, w1_ref[:, c:e])
        b = _dot(h, w3_ref[:, c:e])
        act = (a * _sigmoid(a) * b).astype(BF16)
        acc_ref[...] += _dot(act, w2_ref[c:e, :])


def _dense_ffn_kernel(x_ref, y_ref, wo_ref, g1_ref, sc_ref, sh_ref, g_ref, w1_ref, w3_ref, w2_ref,
                      ea_ref, eb_ref, o_ref, ea_out_ref, eb_out_ref, x1_ref, acc_ref):
    x1_ref[...] = x_ref[0] + g1_ref[0] * _dot(y_ref[0], wo_ref[...])
    h = _norm_mod(x1_ref[...], sc_ref[0], sh_ref[0]).astype(BF16)
    acc_ref[...] = jnp.zeros_like(acc_ref)
    _swiglu_acc(h, w1_ref.at[0], w3_ref.at[0], w2_ref.at[0], acc_ref)
    o_ref[0] = x1_ref[...] + g_ref[0] * acc_ref[...]
    ea_out_ref[...] = ea_ref[0].astype(BF16)
    eb_out_ref[...] = eb_ref[0].astype(BF16)


def _dense_ffn(x, y, w_out, g1, sc, sh, g, w13, w2, layer, moe_w13, moe_w2, moe_layer):
    B, S, _ = x.shape
    steps = B * (S // ROW_TILE)
    n_l = moe_w13.shape[0]
    ea = moe_w13.reshape(n_l, N_EXPERTS * D, 2 * D_FF_EXPERT)
    eb = moe_w2.reshape(n_l, N_EXPERTS * D_FF_EXPERT, D)
    ra, rb = ea.shape[1] // steps, eb.shape[1] // steps
    assert ra * steps == ea.shape[1] and rb * steps == eb.shape[1] and ra % 16 == 0 and rb % 16 == 0
    step = lambda b, i: b * (S // ROW_TILE) + i
    once = pl.Buffered(1)
    return pl.pallas_call(
        _dense_ffn_kernel,
        grid=(B, S // ROW_TILE),
        in_specs=[_row_spec(D), _row_spec(D), pl.BlockSpec((D, D), lambda b, i: (0, 0), pipeline_mode=once),
                  _vec_spec(), _vec_spec(), _vec_spec(), _vec_spec(),
                  pl.BlockSpec((1, D, D_FF), lambda b, i: (layer, 0, 0), pipeline_mode=once),
                  pl.BlockSpec((1, D, D_FF), lambda b, i: (layer, 0, 1), pipeline_mode=once),
                  pl.BlockSpec((1, D_FF, D), lambda b, i: (layer, 0, 0), pipeline_mode=once),
                  pl.BlockSpec((1, ra, ea.shape[2]), lambda b, i: (moe_layer, step(b, i), 0)),
                  pl.BlockSpec((1, rb, D), lambda b, i: (moe_layer, step(b, i), 0))],
        out_specs=[_row_spec(D),
                   pl.BlockSpec((ra, ea.shape[2]), lambda b, i: (step(b, i), 0)),
                   pl.BlockSpec((rb, D), lambda b, i: (step(b, i), 0))],
        out_shape=[jax.ShapeDtypeStruct((B, S, D), F32),
                   jax.ShapeDtypeStruct(ea.shape[1:], BF16),
                   jax.ShapeDtypeStruct(eb.shape[1:], BF16)],
        scratch_shapes=[pltpu.VMEM((ROW_TILE, D), F32), pltpu.VMEM((ROW_TILE, D), F32)],
        compiler_params=_cparams(("arbitrary", "arbitrary")),
        name="dense_ffn",
    )(x, y, w_out, g1, sc, sh, g, w13, w13, w2, ea, eb)


def _moe_ffn_kernel(be_ref, nb_ref, x_ref, w1_ref, w3_ref, w2_ref, o_ref, acc_ref):
    i = pl.program_id(0)
    f = pl.program_id(1)

    @pl.when(f == 0)
    def _():
        acc_ref[...] = jnp.zeros_like(acc_ref)

    @pl.when(i < nb_ref[0])
    def _():
        _swiglu_acc(x_ref[...], w1_ref.at[0], w3_ref.at[0], w2_ref.at[0], acc_ref)

    @pl.when(f == pl.num_programs(1) - 1)
    def _():
        o_ref[...] = acc_ref[...].astype(BF16)


MOE_FF_TILE = 1792


def _moe_ffn(block_e, n_used, xg, w13, w2):
    P = xg.shape[0]
    tf = MOE_FF_TILE
    nf = D_FF_EXPERT // tf
    last = nf - 1

    def held(i, nb):
        return jnp.minimum(i, nb[0] - 1)

    def fcol(i, f, nb):
        return jnp.where(i < nb[0], f, last)

    grid_spec = pltpu.PrefetchScalarGridSpec(
        num_scalar_prefetch=2,
        grid=(P // MOE_TILE, nf),
        in_specs=[
            pl.BlockSpec((MOE_TILE, D), lambda i, f, be, nb: (held(i, nb), 0)),
            pl.BlockSpec((1, D, tf), lambda i, f, be, nb: (be[i], 0, fcol(i, f, nb))),
            pl.BlockSpec((1, D, tf), lambda i, f, be, nb: (be[i], 0, nf + fcol(i, f, nb))),
            pl.BlockSpec((1, tf, D), lambda i, f, be, nb: (be[i], fcol(i, f, nb), 0)),
        ],
        out_specs=pl.BlockSpec((MOE_TILE, D), lambda i, f, be, nb: (i, 0)),
        scratch_shapes=[pltpu.VMEM((MOE_TILE, D), F32)],
    )
    return pl.pallas_call(
        _moe_ffn_kernel,
        grid_spec=grid_spec,
        out_shape=jax.ShapeDtypeStruct((P, D), BF16),
        compiler_params=_cparams(("arbitrary", "arbitrary")),
        name="moe_ffn",
    )(block_e, n_used, xg, w13, w13, w2)


def _router_kernel(x_ref, sc_ref, sh_ref, wh_ref, wl_ref, h_ref, e_ref, g_ref):
    h = _norm_mod(x_ref[0], sc_ref[0], sh_ref[0])
    h_hi = h.astype(BF16)
    h_lo = (h - h_hi.astype(F32)).astype(BF16)
    h_ref[0] = h_hi
    lane = lax.broadcasted_iota(jnp.int32, (1, LANES), 1)
    lanef = lane.astype(F32)
    lg = _dot(h_hi, wh_ref[...]) + (_dot(h_lo, wh_ref[...]) + _dot(h_hi, wl_ref[...]))
    lg = jnp.where(lane < N_EXPERTS, lg, NEG_BIG)
    m1 = jnp.max(lg, axis=-1, keepdims=True)
    i1 = jnp.min(jnp.where(lg == m1, lanef, float(LANES)), axis=-1, keepdims=True)
    lg2 = jnp.where(lanef == i1, NEG_BIG, lg)
    m2 = jnp.max(lg2, axis=-1, keepdims=True)
    i2 = jnp.min(jnp.where(lg2 == m2, lanef, float(LANES)), axis=-1, keepdims=True)
    e = jnp.exp(m2 - m1)
    g1 = 1.0 / (1.0 + e)
    g2 = e / (1.0 + e)
    e_ref[0] = jnp.where(lane == 0, i1, jnp.where(lane == 1, i2, 0.0)).astype(jnp.int32)
    g_ref[0] = jnp.where(lane == 0, g1, jnp.where(lane == 1, g2, 0.0))


def _router(x, sc, sh, w_router):
    B, S, _ = x.shape
    w_hi = w_router.astype(BF16)
    w_lo = (w_router - w_hi.astype(F32)).astype(BF16)
    return pl.pallas_call(
        _router_kernel,
        grid=(B, S // ROW_TILE),
        in_specs=[_row_spec(D), _vec_spec(), _vec_spec(), _full_spec((D, LANES)), _full_spec((D, LANES))],
        out_specs=[_row_spec(D), _row_spec(LANES), _row_spec(LANES)],
        out_shape=[jax.ShapeDtypeStruct((B, S, D), BF16),
                   jax.ShapeDtypeStruct((B, S, LANES), jnp.int32),
                   jax.ShapeDtypeStruct((B, S, LANES), F32)],
        compiler_params=_cparams(("parallel", "parallel")),
        name="router",
    )(x, sc, sh, w_hi, w_lo)


def _combined(x_ref, g_ref, gate_ref, ya_ref, yb_ref):
    gate = gate_ref[0]
    y = ya_ref[0, 0] * gate[:, 0:1] + yb_ref[0, 0] * gate[:, 1:2]
    return x_ref[0] + g_ref[0] * y


def _combine_kernel(x_ref, g_ref, gate_ref, ya_ref, yb_ref, o_ref):
    o_ref[0] = _combined(x_ref, g_ref, gate_ref, ya_ref, yb_ref)


def _combine_norm_kernel(x_ref, g_ref, gate_ref, ya_ref, yb_ref, gf_ref, o_ref):
    x = _combined(x_ref, g_ref, gate_ref, ya_ref, yb_ref)
    ms = jnp.mean(x * x, axis=-1, keepdims=True)
    o_ref[0] = x * lax.rsqrt(ms + EPS) * gf_ref[...]


def _combine(x, g, gates, y_pair, norm_g=None):
    B, S, _ = x.shape
    choice = lambda k: pl.BlockSpec((1, 1, ROW_TILE, D), lambda b, i: (k, b, i, 0))
    in_specs = [_row_spec(D), _vec_spec(), _row_spec(LANES), choice(0), choice(1)]
    args = [x, g, gates, y_pair, y_pair]
    body = _combine_kernel
    if norm_g is not None:
        in_specs.append(_full_spec((1, D)))
        args.append(norm_g.reshape(1, D))
        body = _combine_norm_kernel
    return pl.pallas_call(
        body,
        grid=(B, S // ROW_TILE),
        in_specs=in_specs,
        out_specs=_row_spec(D),
        out_shape=jax.ShapeDtypeStruct((B, S, D), F32),
        compiler_params=_cparams(("parallel", "parallel")),
        name="moe_combine",
    )(*args)


def _pad_cols(w, width):
    return jnp.pad(w, ((0, 0), (0, width - w.shape[1])))


def _gla_mixer(x, sc, sh, w_in, w_gate2, b_gate, gn_g):
    n_main = 2 * GLA_DK + 2 * GLA_DV
    w_main = w_in[:, :n_main].astype(BF16)
    w_glr = _pad_cols(w_in[:, n_main:], LANES).astype(BF16)
    w_g2 = jnp.pad(w_gate2, ((0, LANES - GLA_RANK), (0, 0))).astype(BF16)
    qkvr, log_a = _gla_proj(x, sc, sh, w_main, w_glr, w_g2, b_gate.reshape(1, GLA_DK))
    return _gla_scan(qkvr, log_a, gn_g)


def _conv_layer(x, sc, sh, g, w_in, b_in, dw, dw_b, ln_g, ln_b, w_out, b_out):
    u = _conv_proj(x, sc, sh, w_in.astype(BF16), b_in.reshape(1, 2 * D))
    return _conv_post(u, dw, dw_b.reshape(1, D), ln_g.reshape(1, D), ln_b.reshape(1, D),
                      w_out.astype(BF16), b_out.reshape(1, D), x, g)


def _fox_mixer(x, sc, sh, w_in, b_f, qn_g, kn_g):
    w_main = jnp.concatenate([w_in[:, :3 * D], w_in[:, 3 * D + FOX_HEADS:]], axis=1).astype(BF16)
    w_fl = _pad_cols(jnp.tile(w_in[:, 3 * D:3 * D + FOX_HEADS], (1, FOX_PART_COPIES)), LANES).astype(BF16)
    b_fl = _pad_cols(jnp.tile(b_f.reshape(1, FOX_HEADS), (1, FOX_PART_COPIES)), LANES)
    qkvo, lf = _fox_proj(x, sc, sh, w_main, w_fl, b_fl)
    cum = _fox_cum(lf)
    qa, ka = _fox_prep(qkvo, cum, qn_g, kn_g)
    return _fox_flash(qa, ka, qkvo)


def _moe_layer(x, sc, sh, g, w_router, w13, w2, norm_g):
    B, S, _ = x.shape
    w13 = w13.reshape(N_EXPERTS, D, 2 * D_FF_EXPERT)
    w2 = w2.reshape(N_EXPERTS, D_FF_EXPERT, D)
    N = B * S
    A = 2 * N
    h, e_pad, gates = _router(x, sc, sh, _pad_cols(w_router, LANES))
    e_flat = e_pad[:, :, :2].reshape(A)

    onehot = (e_flat[:, None] == jnp.arange(N_EXPERTS, dtype=jnp.int32)[None, :]).astype(jnp.int32)
    csum = jnp.cumsum(onehot, axis=0)
    rank = jnp.sum(csum * onehot, axis=1) - 1
    counts = csum[-1]
    padded = (counts + MOE_TILE - 1) // MOE_TILE * MOE_TILE
    pad_end = jnp.cumsum(padded)
    pad_start = pad_end - padded
    dest = pad_start[e_flat] + rank
    n_blocks = A // MOE_TILE + N_EXPERTS
    P = n_blocks * MOE_TILE
    block_e = jnp.minimum(
        jnp.searchsorted(pad_end, jnp.arange(n_blocks, dtype=jnp.int32) * MOE_TILE, side='right'),
        N_EXPERTS - 1).astype(jnp.int32)
    n_used = (pad_end[-1] // MOE_TILE).astype(jnp.int32).reshape(1)
    tok = jnp.arange(A, dtype=jnp.int32) // 2
    region_start = jnp.concatenate([pad_start + counts, pad_end[-1:]])
    pads_before = jnp.concatenate([jnp.zeros((1,), jnp.int32), jnp.cumsum(padded - counts)])
    j = jnp.arange(P - A, dtype=jnp.int32)[:, None]
    region = jnp.sum((j >= pads_before[None, 1:]).astype(jnp.int32), axis=1, keepdims=True)
    pick = (region == jnp.arange(N_EXPERTS + 1, dtype=jnp.int32)[None, :]).astype(jnp.int32)
    dummy = jnp.sum(pick * (region_start + j - pads_before), axis=1)
    keys = jnp.concatenate([dest, dummy])
    pad_tok = jnp.arange(P - A, dtype=jnp.int32) % N
    _, slot_tok = lax.sort_key_val(keys, jnp.concatenate([tok, pad_tok]))

    xg = h.reshape(N, D).at[slot_tok].get(mode='promise_in_bounds')
    ys = _moe_ffn(block_e, n_used, xg, w13, w2)
    by_choice = dest.reshape(N, 2).T.reshape(A)
    y_pair = ys.at[by_choice].get(mode='promise_in_bounds').reshape(2, B, S, D)
    return _combine(x, g, gates, y_pair, norm_g)


def kernel(x, c, ada_w, ada_b, gla_w_in, gla_w_gate2, gla_b_gate, gla_gn_g, gla_w_out, conv_w_in, conv_b_in, conv_dw, conv_dw_b, conv_ln_g, conv_ln_b, conv_w_out, conv_b_out, fox_w_in, fox_b_f, fox_qn_g, fox_kn_g, fox_w_out, ffn_w13, ffn_w2, moe_router, moe_w13, moe_w2, norm_f_g):
    depth = ada_w.shape[0]
    B = x.shape[0]
    mod = _ada_mod(c, ada_w, ada_b).reshape(depth, 6, B, 1, D)
    assert depth % 2 == 0
    ffn_w13, ffn_w2 = _cast_bf16(ffn_w13), _cast_bf16(ffn_w2)
    expert_w = None
    for i in range(depth):
        sh1, sc1, g1, sh2, sc2, g2 = [mod[i, j] for j in range(6)]
        m, j = i % 3, i // 3
        dense = i % 2 == 0
        y = w_out = None
        if m == 0:
            y = _gla_mixer(x, sc1, sh1, gla_w_in[j], gla_w_gate2[j], gla_b_gate[j], gla_gn_g[j])
            w_out = gla_w_out[j].astype(BF16)
        elif m == 1:
            x = _conv_layer(x, sc1, sh1, g1, conv_w_in[j], conv_b_in[j], conv_dw[j], conv_dw_b[j],
                            conv_ln_g[j], conv_ln_b[j], conv_w_out[j], conv_b_out[j])
        else:
            y = _fox_mixer(x, sc1, sh1, fox_w_in[j], fox_b_f[j], fox_qn_g[j], fox_kn_g[j])
            w_out = fox_w_out[j].astype(BF16)
        if y is not None and not dense:
            x = _out_proj(y, w_out, jnp.zeros((1, D), F32), x, g1)
        if dense:
            assert y is not None
            x, *expert_w = _dense_ffn(x, y, w_out, g1, sc2, sh2, g2, ffn_w13, ffn_w2, i // 2,
                                      moe_w13, moe_w2, i // 2)
        else:
            x = _moe_layer(x, sc2, sh2, g2, moe_router[i // 2], *expert_w,
                           norm_f_g if i == depth - 1 else None)
    return x
```
